```python
import math
import jax, jax.numpy as jnp
from jax import lax
import numpy as np

D_MODEL = 1024
BATCH = 32
SEQ = 2048
DEPTH = 1

MIX_W = D_MODEL
POOL_W = (3 * D_MODEL) // 8
SSM_W = (3 * D_MODEL) // 8
ATT_W = D_MODEL - POOL_W - SSM_W
IN_W = 2 * MIX_W
POOL_WINDOWS = (2, 4, 8, 16)
POOL_GROUPS = len(POOL_WINDOWS)
POOL_GW = POOL_W // POOL_GROUPS
SSM_GROUP = 16
SSM_NG = SSM_W // SSM_GROUP
SSM_N = 64
DT_MIN = 1e-3
DT_MAX = 1e-1
N_MEM = 256
MEM_HEADS = 4
MEM_HD = ATT_W // MEM_HEADS
EPS = 1e-6

kernel_name = "hybrid_pool_s5_memattn_layer"


def rmsnorm(x, g):
    xf = x.astype(jnp.float32)
    xf = xf * lax.rsqrt(jnp.mean(xf * xf, axis=-1, keepdims=True) + EPS)
    return (xf * g.astype(jnp.float32)).astype(x.dtype)


def pool_mixer(u, w_pool, pool_scale):
    b, l, _ = u.shape
    uf = u.astype(jnp.float32)
    cs0 = jnp.concatenate([jnp.zeros((b, 1, POOL_W), jnp.float32), jnp.cumsum(uf, axis=1)], axis=1)
    pos = jnp.arange(1, l + 1, dtype=jnp.float32)[None, :, None]
    outs = []
    for gi, w in enumerate(POOL_WINDOWS):
        sl = slice(gi * POOL_GW, (gi + 1) * POOL_GW)
        c = cs0[..., sl]
        lower = jnp.concatenate([jnp.zeros((b, w - 1, POOL_GW), jnp.float32), c[:, :l - w + 1]], axis=1)
        mean = (c[:, 1:] - lower) / jnp.minimum(pos, float(w))
        outs.append(jnp.einsum('blc,cd->bld', mean - uf[..., sl], w_pool[gi].astype(jnp.float32)))
    y = jnp.concatenate(outs, axis=-1) * pool_scale.astype(jnp.float32)
    return y.astype(u.dtype)


def _ssm_combine(e1, e2):
    a1, b1 = e1
    a2, b2 = e2
    return a1 * a2, a2 * b1 + b2


def s5_mixer(u, a_re, a_im, log_dt, b_re, b_im, c_re, c_im, d_skip, w_glu):
    bsz, l, _ = u.shape
    f32 = jnp.float32
    uf = u.astype(f32).reshape(bsz, l, SSM_NG, SSM_GROUP)
    lam = lax.complex(a_re.astype(f32), a_im.astype(f32))
    dt = jnp.exp(log_dt.astype(f32))[:, None]
    lam_bar = jnp.exp(lam * dt)
    b_mat = lax.complex(b_re.astype(f32), b_im.astype(f32))
    c_mat = lax.complex(c_re.astype(f32), c_im.astype(f32))
    b_bar = ((lam_bar - 1.0) / lam)[..., None] * b_mat
    bu = jnp.einsum('blgc,gnc->blgn', uf.astype(jnp.complex64), b_bar)
    lam_all = jnp.broadcast_to(lam_bar, bu.shape)
    _, hs = lax.associative_scan(_ssm_combine, (lam_all, bu), axis=1)
    y = jnp.einsum('blgn,gcn->blgc', hs, c_mat).real + d_skip.astype(f32).reshape(SSM_NG, SSM_GROUP) * uf
    y = jax.nn.gelu(y.reshape(bsz, l, SSM_W))
    z = y @ w_glu.astype(f32)
    out = z[..., :SSM_W] * jax.nn.sigmoid(z[..., SSM_W:])
    return out.astype(u.dtype)


def memory_attention(q, mem, g_mem, w_kv):
    bsz, l, _ = q.shape
    m = rmsnorm(mem, g_mem)
    kv = m @ w_kv
    k = kv[..., :ATT_W].reshape(bsz, -1, MEM_HEADS, MEM_HD).astype(jnp.float32)
    v = kv[..., ATT_W:].reshape(bsz, -1, MEM_HEADS, MEM_HD).astype(jnp.float32)
    qh = q.reshape(bsz, l, MEM_HEADS, MEM_HD).astype(jnp.float32)
    s = jnp.einsum('blhd,bmhd->bhlm', qh, k) * (MEM_HD ** -0.5)
    p = jax.nn.softmax(s, axis=-1)
    o = jnp.einsum('bhlm,bmhd->blhd', p, v).reshape(bsz, l, ATT_W)
    return o.astype(q.dtype)


def setup_inputs(seed: int = 0) -> dict:
    key = jax.random.key(seed)
    ks = jax.random.split(key, 20)
    f32 = jnp.float32
    nrm = lambda k, shape, s: jax.random.normal(k, shape, f32) * s
    n_idx = jnp.arange(SSM_N, dtype=f32)
    a_re = -0.5 + nrm(ks[5], (DEPTH, SSM_NG, SSM_N), 1e-2)
    a_im = math.pi * n_idx[None, None, :] + nrm(ks[6], (DEPTH, SSM_NG, SSM_N), 1e-2)
    log_dt = jax.random.uniform(ks[7], (DEPTH, SSM_NG), f32, math.log(DT_MIN), math.log(DT_MAX))
    return {
        "x": nrm(ks[0], (BATCH, SEQ, D_MODEL), 1.0),
        "mem": nrm(ks[1], (BATCH, N_MEM, D_MODEL), 1.0),
        "g_pre": 1.0 + nrm(ks[2], (DEPTH, D_MODEL), 0.02),
        "w_in": nrm(ks[3], (DEPTH, D_MODEL, IN_W), D_MODEL ** -0.5),
        "w_pool": nrm(ks[4], (DEPTH, POOL_GROUPS, POOL_GW, POOL_GW), POOL_GW ** -0.5),
        "pool_scale": 1.0 + nrm(ks[8], (DEPTH, POOL_W), 0.02),
        "a_re": a_re,
        "a_im": a_im,
        "log_dt": log_dt,
        "b_re": nrm(ks[9], (DEPTH, SSM_NG, SSM_N, SSM_GROUP), (2 * SSM_GROUP) ** -0.5),
        "b_im": nrm(ks[10], (DEPTH, SSM_NG, SSM_N, SSM_GROUP), (2 * SSM_GROUP) ** -0.5),
        "c_re": nrm(ks[11], (DEPTH, SSM_NG, SSM_GROUP, SSM_N), (2 * SSM_N) ** -0.5),
        "c_im": nrm(ks[12], (DEPTH, SSM_NG, SSM_GROUP, SSM_N), (2 * SSM_N) ** -0.5),
        "d_skip": nrm(ks[13], (DEPTH, SSM_W), 1.0),
        "w_glu": nrm(ks[14], (DEPTH, SSM_W, 2 * SSM_W), SSM_W ** -0.5),
        "g_mem": 1.0 + nrm(ks[15], (DEPTH, D_MODEL), 0.02),
        "w_kv": nrm(ks[16], (DEPTH, D_MODEL, 2 * ATT_W), D_MODEL ** -0.5),
        "w_out": nrm(ks[17], (DEPTH, MIX_W, D_MODEL), MIX_W ** -0.5),
        "g_post": 1.0 + nrm(ks[18], (DEPTH, D_MODEL), 0.02),
    }


def reference(x, mem, g_pre, w_in, w_pool, pool_scale, a_re, a_im, log_dt, b_re, b_im,
              c_re, c_im, d_skip, w_glu, g_mem, w_kv, w_out, g_post):
    for i in range(DEPTH):
        h = rmsnorm(x, g_pre[i])
        proj = h @ w_in[i]
        val, gate = proj[..., :MIX_W], proj[..., MIX_W:]
        u_pool = val[..., :POOL_W]
        u_ssm = val[..., POOL_W:POOL_W + SSM_W]
        q = val[..., POOL_W + SSM_W:]
        y_pool = pool_mixer(u_pool, w_pool[i], pool_scale[i])
        y_ssm = s5_mixer(u_ssm, a_re[i], a_im[i], log_dt[i], b_re[i], b_im[i],
                         c_re[i], c_im[i], d_skip[i], w_glu[i])
        y_att = memory_attention(q, mem, g_mem[i], w_kv[i])
        y = jnp.concatenate([y_pool, y_ssm, y_att], axis=-1) * jax.nn.silu(gate)
        out = y @ w_out[i]
        x = x + rmsnorm(out, g_post[i])
    return x
```

```python
import functools

import jax
import jax.numpy as jnp
from jax import lax
from jax.experimental import pallas as pl
from jax.experimental.pallas import tpu as pltpu

LANES = 128
NB = 8
TT = 64
TC = 4
TB = 16
HALO = 16
POOL_WINDOWS = (2, 4, 8, 16)
SSM_GROUP = 16
MEM_HEADS = 4
EPS = 1e-6
VMEM_LIMIT_BYTES = 56 * 1024 * 1024

F32 = jnp.float32
BF16 = jnp.bfloat16


def _sigmoid(v):
    return 1.0 / (1.0 + jnp.exp(-v))


def _dot(a, b):
    return jnp.dot(a, b, preferred_element_type=F32)


def _kv_kernel(mem_ref, gmem_ref, wkt_ref, wv_ref, kbd_ref, vbd_ref, *, att_w, n_mem):
    m = mem_ref[0]
    ms = jnp.mean(m * m, axis=-1, keepdims=True)
    mn = ((m * lax.rsqrt(ms + EPS)) * gmem_ref[...]).astype(BF16)
    kt = lax.dot_general(wkt_ref[...], mn, (((1,), (1,)), ((), ())), preferred_element_type=F32)
    v = _dot(mn, wv_ref[...])
    hd = att_w // MEM_HEADS
    kt = kt * (hd ** -0.5)
    row_head = lax.broadcasted_iota(jnp.int32, (att_w, n_mem), 0) // hd
    col_head = lax.broadcasted_iota(jnp.int32, (n_mem, att_w), 1) // hd
    for hh in range(MEM_HEADS):
        kbd_ref[0, :, hh * n_mem:(hh + 1) * n_mem] = jnp.where(row_head == hh, kt, 0.0).astype(BF16)
        vbd_ref[0, hh * n_mem:(hh + 1) * n_mem, :] = jnp.where(col_head == hh, v, 0.0).astype(BF16)


def _kv_call(mem, g_mem, wkt, wv):
    bsz, n_mem, d = mem.shape
    att_w = wv.shape[1]
    kern = functools.partial(_kv_kernel, att_w=att_w, n_mem=n_mem)
    return pl.pallas_call(
        kern,
        grid=(bsz,),
        in_specs=[
            pl.BlockSpec((1, n_mem, d), lambda b: (b, 0, 0)),
            pl.BlockSpec((1, d), lambda b: (0, 0)),
            pl.BlockSpec((att_w, d), lambda b: (0, 0)),
            pl.BlockSpec((d, att_w), lambda b: (0, 0)),
        ],
        out_specs=[
            pl.BlockSpec((1, att_w, MEM_HEADS * n_mem), lambda b: (b, 0, 0)),
            pl.BlockSpec((1, MEM_HEADS * n_mem, att_w), lambda b: (b, 0, 0)),
        ],
        out_shape=[
            jax.ShapeDtypeStruct((bsz, att_w, MEM_HEADS * n_mem), BF16),
            jax.ShapeDtypeStruct((bsz, MEM_HEADS * n_mem, att_w), BF16),
        ],
        compiler_params=pltpu.CompilerParams(dimension_semantics=("arbitrary",)),
        name="kv_call",
    )(mem, g_mem, wkt, wv)


def _layer_kernel(x_ref, kbd_ref, vbd_ref, gpre_ref, win_ref, wpool_ref, pscale_ref, invw_ref, invc0_ref,
                  wtb_ref, wc_ref, a_ref, dskip_ref, wglu_ref, wout_ref, gpost_ref,
                  out_ref,
                  utb_ref, ytb_ref, gate_ref, q_ref, ycat_ref, pd_ref, g_ref, hin_ref, hc_ref,
                  *, d, pool_w, ssm_w, att_w, n_mem):
    ti = pl.program_id(1)
    m_rows = NB * TT
    n_pool = pool_w // LANES
    n_ssm = ssm_w // LANES
    n_slab = n_pool + n_ssm
    mix_w = pool_w + ssm_w
    n_chunk = TT // TC
    st = hc_ref.shape[-1] // 2

    @pl.when(ti == 0)
    def _():
        utb_ref[0:n_pool, 0:HALO * NB, :] = jnp.zeros((n_pool, HALO * NB, LANES), F32)
        hc_ref[...] = jnp.zeros(hc_ref.shape, F32)

    x = x_ref[...].reshape(m_rows, d)
    ms = jnp.mean(x * x, axis=-1, keepdims=True)
    h = ((x * lax.rsqrt(ms + EPS)) * gpre_ref[...]).astype(BF16)

    val = _dot(h, win_ref[:, 0:mix_w])
    for b in range(NB):
        for k in range(n_slab):
            utb_ref[k, pl.ds(HALO * NB + b, TT, stride=NB), :] = val[b * TT:(b + 1) * TT, k * LANES:(k + 1) * LANES]
    q_ref[...] = _dot(h, win_ref[:, mix_w:mix_w + att_w]).astype(BF16)
    gate = _dot(h, win_ref[:, mix_w + att_w:])
    gate_ref[...] = gate * _sigmoid(gate)

    gw = pool_w // len(POOL_WINDOWS)
    lane = lax.broadcasted_iota(jnp.int32, (TB * NB, LANES), 1)
    for k in range(n_pool):
        w_lo = POOL_WINDOWS[(k * LANES) // gw]
        w_hi = POOL_WINDOWS[min(((k + 1) * LANES - 1) // gw, len(POOL_WINDOWS) - 1)]
        split = ((k * LANES) // gw + 1) * gw - k * LANES
        for i in range(TT // TB):
            base = (HALO + i * TB) * NB
            ext = utb_ref[k, base - (HALO - 1) * NB: base + TB * NB, :]
            sums = {1: ext}
            w = 1
            while w < w_hi:
                prev = sums[w]
                sums[2 * w] = prev[w * NB:] + prev[:-w * NB]
                w *= 2
            cur = ext[(HALO - 1) * NB:]
            lo = sums[w_lo][(HALO - w_lo) * NB:]
            hi = sums[w_hi][(HALO - w_hi) * NB:]
            ssum = jnp.where(lane < split, lo, hi)
            scale = invw_ref[:, k * LANES:(k + 1) * LANES]
            if i == 0:
                scale = jnp.where(ti == 0, invc0_ref[:, k * LANES:(k + 1) * LANES], scale)
            pd_ref[i * TB * NB:(i + 1) * TB * NB, k * LANES:(k + 1) * LANES] = (ssum * scale - cur).astype(BF16)
    ypool = _dot(pd_ref[...], wpool_ref[...]) * pscale_ref[...]
    for k in range(n_pool):
        ytb_ref[k] = ypool[:, k * LANES:(k + 1) * LANES]

    y_act = []
    for j in range(n_ssm):
        u_tb = utb_ref[n_pool + j, HALO * NB:(HALO + TT) * NB, :]
        u4 = u_tb.reshape(n_chunk, TC, NB, LANES)
        up = jnp.concatenate([u4[:, s].reshape(n_chunk * NB, LANES) for s in range(TC)], axis=1).astype(BF16)
        yg = _dot(up, wtb_ref[j])
        y_intra = yg[:, 0:TC * LANES]
        g_ref[...] = yg[:, TC * LANES:]
        a_re = a_ref[j, :, 0:st]
        a_im = a_ref[j, :, st:]
        h_re = hc_ref[j, :, 0:st]
        h_im = hc_ref[j, :, st:]
        for r in range(n_chunk):
            rows = slice(r * NB, (r + 1) * NB)
            hin_ref[rows, 0:st] = h_re
            hin_ref[rows, st:] = h_im
            g_re = g_ref[rows, 0:st]
            g_im = g_ref[rows, st:]
            h_re, h_im = a_re * h_re - a_im * h_im + g_re, a_re * h_im + a_im * h_re + g_im
        hc_ref[j, :, 0:st] = h_re
        hc_ref[j, :, st:] = h_im
        y_lin = y_intra + _dot(hin_ref[...].astype(BF16), wc_ref[j])
        y_tb = jnp.stack([y_lin[:, s * LANES:(s + 1) * LANES].reshape(n_chunk, NB, LANES) for s in range(TC)],
                         axis=1).reshape(TT * NB, LANES)
        y = y_tb + dskip_ref[:, j * LANES:(j + 1) * LANES] * u_tb
        y_act.append(jax.nn.gelu(y).astype(BF16))
    z = _dot(jnp.concatenate(y_act, axis=1), wglu_ref[...])
    y_ssm = z[:, 0:ssm_w] * _sigmoid(z[:, ssm_w:])
    for j in range(n_ssm):
        ytb_ref[n_pool + j] = y_ssm[:, j * LANES:(j + 1) * LANES]

    hd = att_w // MEM_HEADS
    lane_t = lax.broadcasted_iota(jnp.int32, (TT, LANES), 1)
    for b in range(NB):
        rows = slice(b * TT, (b + 1) * TT)
        for k in range(n_slab):
            cols = slice(k * LANES, (k + 1) * LANES)
            yk = ytb_ref[k, pl.ds(b, TT, stride=NB), :]
            ycat_ref[rows, cols] = (yk * gate_ref[rows, cols]).astype(BF16)
        s = _dot(q_ref[rows, :], kbd_ref[b])
        es, inv = [], []
        for hh in range(MEM_HEADS):
            sh = s[:, hh * n_mem:(hh + 1) * n_mem]
            e = jnp.exp(sh - jnp.max(sh, axis=-1, keepdims=True))
            inv.append(1.0 / jnp.sum(e, axis=-1, keepdims=True))
            es.append(e.astype(BF16))
        o = _dot(jnp.concatenate(es, axis=1), vbd_ref[b])
        heads_per_vreg = LANES // hd
        norm = []
        for c in range(att_w // LANES):
            sc = inv[c * heads_per_vreg]
            for e_i in range(1, heads_per_vreg):
                sc = jnp.where(lane_t < e_i * hd, sc, inv[c * heads_per_vreg + e_i])
            norm.append(sc)
        o = o * jnp.concatenate(norm, axis=1)
        ycat_ref[rows, mix_w:] = (o * gate_ref[rows, mix_w:]).astype(BF16)

    out = _dot(ycat_ref[...], wout_ref[...])
    ms2 = jnp.mean(out * out, axis=-1, keepdims=True)
    res = x_ref[...].reshape(m_rows, d) + (out * lax.rsqrt(ms2 + EPS)) * gpost_ref[...]
    out_ref[...] = res.reshape(NB, TT, d)

    for k in range(n_pool):
        utb_ref[k, 0:HALO * NB, :] = utb_ref[k, TT * NB:(TT + HALO) * NB, :]


def _const_spec(shape):
    zeros = (0,) * len(shape)
    return pl.BlockSpec(shape, lambda bi, ti: zeros, pipeline_mode=pl.Buffered(1))


def _layer_call(x, kbd, vbd, g_pre, w_in, wpool_bd, pscale, inv_w, inv_c0, wtb, wc, a_mat, d_skip, w_glu, w_out, g_post,
                *, pool_w, ssm_w, att_w, n_mem):
    bsz, seq, d = x.shape
    assert bsz % NB == 0 and seq % TT == 0 and TT % TB == 0 and TT % TC == 0 and TB == HALO
    assert pool_w % LANES == 0 and ssm_w % LANES == 0 and att_w % LANES == 0
    n_pool, n_ssm = pool_w // LANES, ssm_w // LANES
    n_slab = n_pool + n_ssm
    n_state = a_mat.shape[-1]
    m_rows = NB * TT
    kern = functools.partial(_layer_kernel, d=d, pool_w=pool_w, ssm_w=ssm_w, att_w=att_w, n_mem=n_mem)
    consts = (g_pre, w_in, wpool_bd, pscale, inv_w, inv_c0, wtb, wc, a_mat, d_skip, w_glu, w_out, g_post)
    return pl.pallas_call(
        kern,
        grid=(bsz // NB, seq // TT),
        in_specs=[
            pl.BlockSpec((NB, TT, d), lambda bi, ti: (bi, ti, 0)),
            pl.BlockSpec((NB,) + kbd.shape[1:], lambda bi, ti: (bi, 0, 0), pipeline_mode=pl.Buffered(1)),
            pl.BlockSpec((NB,) + vbd.shape[1:], lambda bi, ti: (bi, 0, 0), pipeline_mode=pl.Buffered(1)),
        ] + [_const_spec(c.shape) for c in consts],
        out_specs=pl.BlockSpec((NB, TT, d), lambda bi, ti: (bi, ti, 0)),
        out_shape=jax.ShapeDtypeStruct(x.shape, x.dtype),
        scratch_shapes=[
            pltpu.VMEM((n_slab, (HALO + TT) * NB, LANES), F32),
            pltpu.VMEM((n_slab, TT * NB, LANES), F32),
            pltpu.VMEM((m_rows, d), F32),
            pltpu.VMEM((m_rows, att_w), BF16),
            pltpu.VMEM((m_rows, d), BF16),
            pltpu.VMEM((m_rows, pool_w), BF16),
            pltpu.VMEM((m_rows // TC, n_state), F32),
            pltpu.VMEM((m_rows // TC, n_state), F32),
            pltpu.VMEM((n_ssm, NB, n_state), F32),
        ],
        compiler_params=pltpu.CompilerParams(
            dimension_semantics=("arbitrary", "arbitrary"),
            vmem_limit_bytes=VMEM_LIMIT_BYTES,
        ),
        name="layer_call",
    )(x, kbd, vbd, *consts)


def _ssm_matrices(a_re, a_im, log_dt, b_re, b_im, c_re, c_im):
    n_grp, n_st = a_re.shape
    c = b_re.shape[-1]
    gpb = LANES // c
    nj = n_grp // gpb
    lam = lax.complex(a_re, a_im)
    dt = jnp.exp(log_dt)[:, None]
    ks = jnp.arange(TC + 1, dtype=F32)
    p = jnp.exp((lam * dt)[None] * ks[:, None, None])
    lam_bar = jnp.exp(lam * dt)
    b_bar = ((lam_bar - 1.0) / lam)[..., None] * lax.complex(b_re, b_im)
    c_mat = lax.complex(c_re, c_im)
    eye = jnp.eye(gpb, dtype=F32)
    hi = lax.Precision.HIGHEST
    kern = jnp.einsum('gon,kgn,gni->kgoi', c_mat, p[:TC], b_bar, precision=hi).real
    lag = jnp.arange(TC)[None, :] - jnp.arange(TC)[:, None]
    kst = jnp.where((lag >= 0)[:, :, None, None, None], kern[jnp.clip(lag, 0, TC - 1)], 0.0)
    kst = kst.reshape(TC, TC, nj, gpb, c, c)
    wt = jnp.einsum('stjgoi,gh->jsgitho', kst, eye, precision=hi).reshape(nj, TC * LANES, TC * LANES)
    pb = p[TC - 1 - jnp.arange(TC)][..., None] * b_bar[None]
    pb = pb.reshape(TC, nj, gpb, n_st, c)
    pb = jnp.stack([pb.real, pb.imag], 0)
    wb = jnp.einsum('psjgni,gh->jsgiphn', pb, eye, precision=hi).reshape(nj, TC * LANES, 2 * gpb * n_st)
    cp = c_mat[None] * p[1:TC + 1][:, :, None, :]
    cp = cp.reshape(TC, nj, gpb, c, n_st)
    cp = jnp.stack([cp.real, -cp.imag], 0)
    wc = jnp.einsum('psjgon,gh->jpgnsho', cp, eye, precision=hi).reshape(nj, 2 * gpb * n_st, TC * LANES)
    a_chunk = p[TC].reshape(nj, gpb * n_st)
    a_mat = jnp.concatenate([a_chunk.real, a_chunk.imag], axis=-1)
    a_mat = jnp.broadcast_to(a_mat[:, None, :], (nj, NB, a_mat.shape[-1]))
    wtb = jnp.concatenate([wt, wb], axis=-1).astype(BF16)
    return wtb, wc.astype(BF16), a_mat


def _pool_constants(w_pool, pool_w):
    n_grp, gw, _ = w_pool.shape
    wbd = jnp.zeros((pool_w, pool_w), F32)
    for gi in range(n_grp):
        wbd = wbd.at[gi * gw:(gi + 1) * gw, gi * gw:(gi + 1) * gw].set(w_pool[gi])
    win = jnp.repeat(jnp.asarray(POOL_WINDOWS, F32), gw)
    inv_w = (1.0 / win)[None, :]
    pos = jnp.repeat(jnp.arange(1, HALO + 1, dtype=F32), NB)[:, None]
    inv_c0 = 1.0 / jnp.minimum(pos, win[None, :])
    return wbd.astype(BF16), inv_w, inv_c0


def _layer(x, mem, g_pre, w_in, w_pool, pool_scale, a_re, a_im, log_dt, b_re, b_im, c_re, c_im, d_skip, w_glu,
           g_mem, w_kv, w_out, g_post):
    d = x.shape[-1]
    n_mem = mem.shape[1]
    pool_w = w_pool.shape[0] * w_pool.shape[1]
    ssm_w = d_skip.shape[0]
    att_w = w_kv.shape[1] // 2
    assert a_re.shape[0] * SSM_GROUP == ssm_w and b_re.shape[-1] == SSM_GROUP
    assert pool_w + ssm_w + att_w == d and w_pool.shape[0] == len(POOL_WINDOWS)

    kbd, vbd = _kv_call(mem, g_mem[None, :], w_kv[:, :att_w].T.astype(BF16), w_kv[:, att_w:].astype(BF16))
    wtb, wc, a_mat = _ssm_matrices(a_re, a_im, log_dt, b_re, b_im, c_re, c_im)
    wpool_bd, inv_w, inv_c0 = _pool_constants(w_pool, pool_w)
    return _layer_call(
        x, kbd, vbd, g_pre[None, :], w_in.astype(BF16), wpool_bd, pool_scale[None, :], inv_w, inv_c0,
        wtb, wc, a_mat, d_skip[None, :], w_glu.astype(BF16), w_out.astype(BF16), g_post[None, :],
        pool_w=pool_w, ssm_w=ssm_w, att_w=att_w, n_mem=n_mem)


def kernel(x, mem, g_pre, w_in, w_pool, pool_scale, a_re, a_im, log_dt, b_re, b_im, c_re, c_im, d_skip, w_glu, g_mem, w_kv, w_out, g_post):
    for i in range(g_pre.shape[0]):
        x = _layer(x, mem, g_pre[i], w_in[i], w_pool[i], pool_scale[i], a_re[i], a_im[i], log_dt[i], b_re[i], b_im[i],
                   c_re[i], c_im[i], d_skip[i], w_glu[i], g_mem[i], w_kv[i], w_out[i], g_post[i])
    return x
```

```python
import functools

import jax
import jax.numpy as jnp
from jax import lax
from jax.experimental import pallas as pl
from jax.experimental.pallas import tpu as pltpu

LANES = 128
NB = 8
TT = 64
TC = 4
BC = 2
TB = 16
HALO = 16
POOL_WINDOWS = (2, 4, 8, 16)
SSM_GROUP = 16
MEM_HEADS = 4
EPS = 1e-6
VMEM_LIMIT_BYTES = 56 * 1024 * 1024

F32 = jnp.float32
BF16 = jnp.bfloat16


def _sigmoid(v):
    return 0.5 * jnp.tanh(0.5 * v) + 0.5


def _kv_kernel(mem_ref, gmem_ref, wkt_ref, wv_ref, kbd_ref, vbd_ref, *, att_w, n_mem):
    m = mem_ref[0]
    ms = jnp.mean(m * m, axis=-1, keepdims=True)
    mn = ((m * lax.rsqrt(ms + EPS)) * gmem_ref[...]).astype(BF16)
    kt = lax.dot_general(wkt_ref[...], mn, (((1,), (1,)), ((), ())), preferred_element_type=F32)
    v = jnp.dot(mn, wv_ref[...], preferred_element_type=F32)
    hd = att_w // MEM_HEADS
    kt = kt * (hd ** -0.5)
    row_head = lax.broadcasted_iota(jnp.int32, (att_w, n_mem), 0) // hd
    col_head = lax.broadcasted_iota(jnp.int32, (n_mem, att_w), 1) // hd
    for hh in range(MEM_HEADS):
        kbd_ref[0, :, hh * n_mem:(hh + 1) * n_mem] = jnp.where(row_head == hh, kt, 0.0).astype(BF16)
        vbd_ref[0, hh * n_mem:(hh + 1) * n_mem, :] = jnp.where(col_head == hh, v, 0.0).astype(BF16)


def _kv_call(mem, g_mem, wkt, wv):
    bsz, n_mem, d = mem.shape
    att_w = wv.shape[1]
    kern = functools.partial(_kv_kernel, att_w=att_w, n_mem=n_mem)
    return pl.pallas_call(
        kern,
        grid=(bsz,),
        in_specs=[
            pl.BlockSpec((1, n_mem, d), lambda b: (b, 0, 0)),
            pl.BlockSpec((1, d), lambda b: (0, 0)),
            pl.BlockSpec((att_w, d), lambda b: (0, 0)),
            pl.BlockSpec((d, att_w), lambda b: (0, 0)),
        ],
        out_specs=[
            pl.BlockSpec((1, att_w, MEM_HEADS * n_mem), lambda b: (b, 0, 0)),
            pl.BlockSpec((1, MEM_HEADS * n_mem, att_w), lambda b: (b, 0, 0)),
        ],
        out_shape=[
            jax.ShapeDtypeStruct((bsz, att_w, MEM_HEADS * n_mem), BF16),
            jax.ShapeDtypeStruct((bsz, MEM_HEADS * n_mem, att_w), BF16),
        ],
        compiler_params=pltpu.CompilerParams(dimension_semantics=("arbitrary",)),
        name="kv_call",
    )(mem, g_mem, wkt, wv)


def _layer_kernel(x_ref, kbd_ref, vbd_ref, gpre_ref, win_ref, wpool_ref, pscale_ref, invw_ref, invc0_ref,
                  wtb_ref, wc_ref, a_ref, dskip_ref, wglu_ref, wout_ref, gpost_ref,
                  out_ref,
                  utb_ref, ytb_ref, gate_ref, q_ref, ycat_ref, pd_ref, g_ref, hin_ref, hc_ref,
                  s_ref, win_s, wout_s, kbd_s,
                  *, d, pool_w, ssm_w, att_w, n_mem):
    bi = pl.program_id(0)
    ti = pl.program_id(1)
    n_pool = pool_w // LANES
    n_ssm = ssm_w // LANES
    n_slab = n_pool + n_ssm
    mix_w = pool_w + ssm_w
    n_chunk = TT // TC
    st = hc_ref.shape[-1] // 2
    hd = att_w // MEM_HEADS

    @pl.when((bi == 0) & (ti == 0))
    def _():
        win_s[...] = win_ref[...]
        wout_s[...] = wout_ref[...]

    @pl.when(ti == 0)
    def _():
        kbd_s[...] = kbd_ref[...]
        utb_ref[0:n_pool, 0:HALO * NB, :] = jnp.zeros((n_pool, HALO * NB, LANES), F32)
        hc_ref[...] = jnp.zeros(hc_ref.shape, F32)

    for c in range(NB // BC):
        crow = slice(c * BC * TT, (c + 1) * BC * TT)
        x = x_ref[c * BC:(c + 1) * BC].reshape(BC * TT, d)
        ms = jnp.mean(x * x, axis=-1, keepdims=True)
        h = ((x * lax.rsqrt(ms + EPS)) * gpre_ref[...]).astype(BF16)
        val = jnp.dot(h, win_s[:, 0:mix_w], preferred_element_type=F32)
        for bb in range(BC):
            for k in range(n_slab):
                utb_ref[k, pl.ds(HALO * NB + c * BC + bb, TT, stride=NB), :] = (
                    val[bb * TT:(bb + 1) * TT, k * LANES:(k + 1) * LANES])
        q_ref[crow, :] = jnp.dot(h, win_s[:, mix_w:mix_w + att_w], preferred_element_type=F32).astype(BF16)
        gate = jnp.dot(h, win_s[:, mix_w + att_w:], preferred_element_type=F32)
        gate_ref[crow, :] = gate * _sigmoid(gate)

    u_tb, y_intra = [], []
    for j in range(n_ssm):
        u = utb_ref[n_pool + j, HALO * NB:(HALO + TT) * NB, :]
        u4 = u.reshape(n_chunk, TC, NB, LANES)
        up = jnp.concatenate([u4[:, s].reshape(n_chunk * NB, LANES) for s in range(TC)], axis=1).astype(BF16)
        yg = jnp.dot(up, wtb_ref[j], preferred_element_type=F32)
        y_intra.append(yg[:, 0:TC * LANES])
        g_ref[j] = yg[:, TC * LANES:]
        u_tb.append(u)

    for b in range(NB):
        rows = slice(b * TT, (b + 1) * TT)
        s_ref[rows, :] = jnp.dot(q_ref[rows, :], kbd_s[b], preferred_element_type=F32)

    gw = pool_w // len(POOL_WINDOWS)
    lane = lax.broadcasted_iota(jnp.int32, (TB * NB, LANES), 1)
    for k in range(n_pool):
        w_lo = POOL_WINDOWS[(k * LANES) // gw]
        w_hi = POOL_WINDOWS[min(((k + 1) * LANES - 1) // gw, len(POOL_WINDOWS) - 1)]
        split = ((k * LANES) // gw + 1) * gw - k * LANES
        for i in range(TT // TB):
            base = (HALO + i * TB) * NB
            ext = utb_ref[k, base - (HALO - 1) * NB: base + TB * NB, :]
            sums = {1: ext}
            w = 1
            while w < w_hi:
                prev = sums[w]
                sums[2 * w] = prev[w * NB:] + prev[:-w * NB]
                w *= 2
            cur = ext[(HALO - 1) * NB:]
            lo = sums[w_lo][(HALO - w_lo) * NB:]
            hi = sums[w_hi][(HALO - w_hi) * NB:]
            ssum = jnp.where(lane < split, lo, hi)
            scale = invw_ref[:, k * LANES:(k + 1) * LANES]
            if i == 0:
                scale = jnp.where(ti == 0, invc0_ref[:, k * LANES:(k + 1) * LANES], scale)
            pd_ref[i * TB * NB:(i + 1) * TB * NB, k * LANES:(k + 1) * LANES] = (ssum * scale - cur).astype(BF16)
    ypool = jnp.dot(pd_ref[...], wpool_ref[...], preferred_element_type=F32) * pscale_ref[...]
    for k in range(n_pool):
        ytb_ref[k] = ypool[:, k * LANES:(k + 1) * LANES]

    a_re = [a_ref[j, :, 0:st] for j in range(n_ssm)]
    a_im = [a_ref[j, :, st:] for j in range(n_ssm)]
    h_re = [hc_ref[j, :, 0:st] for j in range(n_ssm)]
    h_im = [hc_ref[j, :, st:] for j in range(n_ssm)]
    for r in range(n_chunk):
        rows = slice(r * NB, (r + 1) * NB)
        for j in range(n_ssm):
            hin_ref[j, rows, 0:st] = h_re[j]
            hin_ref[j, rows, st:] = h_im[j]
            g_re = g_ref[j, rows, 0:st]
            g_im = g_ref[j, rows, st:]
            h_re[j], h_im[j] = (a_re[j] * h_re[j] - a_im[j] * h_im[j] + g_re,
                                a_re[j] * h_im[j] + a_im[j] * h_re[j] + g_im)
    for j in range(n_ssm):
        hc_ref[j, :, 0:st] = h_re[j]
        hc_ref[j, :, st:] = h_im[j]

    y_act = []
    for j in range(n_ssm):
        y_lin = y_intra[j] + jnp.dot(hin_ref[j].astype(BF16), wc_ref[j], preferred_element_type=F32)
        y_tb = jnp.stack([y_lin[:, s * LANES:(s + 1) * LANES].reshape(n_chunk, NB, LANES) for s in range(TC)],
                         axis=1).reshape(TT * NB, LANES)
        y = y_tb + dskip_ref[:, j * LANES:(j + 1) * LANES] * u_tb[j]
        y_act.append(jax.nn.gelu(y).astype(BF16))

    lane_t = lax.broadcasted_iota(jnp.int32, (TT, LANES), 1)
    heads_per_vreg = LANES // hd
    for b in range(NB):
        rows = slice(b * TT, (b + 1) * TT)
        es, inv = [], []
        for hh in range(MEM_HEADS):
            sh = s_ref[rows, hh * n_mem:(hh + 1) * n_mem]
            e = jnp.exp(sh - jnp.max(sh, axis=-1, keepdims=True))
            inv.append(1.0 / jnp.sum(e, axis=-1, keepdims=True))
            es.append(e.astype(BF16))
        o = jnp.dot(jnp.concatenate(es, axis=1), vbd_ref[b], preferred_element_type=F32)
        norm = []
        for c in range(att_w // LANES):
            sc = inv[c * heads_per_vreg]
            for e_i in range(1, heads_per_vreg):
                sc = jnp.where(lane_t < e_i * hd, sc, inv[c * heads_per_vreg + e_i])
            norm.append(sc)
        o = o * jnp.concatenate(norm, axis=1)
        ycat_ref[rows, mix_w:] = (o * gate_ref[rows, mix_w:]).astype(BF16)

    z = jnp.dot(jnp.concatenate(y_act, axis=1), wglu_ref[...], preferred_element_type=F32)
    y_ssm = z[:, 0:ssm_w] * _sigmoid(z[:, ssm_w:])
    for j in range(n_ssm):
        ytb_ref[n_pool + j] = y_ssm[:, j * LANES:(j + 1) * LANES]

    for c in range(NB // BC):
        crow = slice(c * BC * TT, (c + 1) * BC * TT)
        for bb in range(BC):
            b = c * BC + bb
            rows = slice(b * TT, (b + 1) * TT)
            for k in range(n_slab):
                cols = slice(k * LANES, (k + 1) * LANES)
                yk = ytb_ref[k, pl.ds(b, TT, stride=NB), :]
                ycat_ref[rows, cols] = (yk * gate_ref[rows, cols]).astype(BF16)
        out = jnp.dot(ycat_ref[crow, :], wout_s[...], preferred_element_type=F32)
        ms2 = jnp.mean(out * out, axis=-1, keepdims=True)
        res = x_ref[c * BC:(c + 1) * BC].reshape(BC * TT, d) + (out * lax.rsqrt(ms2 + EPS)) * gpost_ref[...]
        out_ref[c * BC:(c + 1) * BC] = res.reshape(BC, TT, d)

    for k in range(n_pool):
        utb_ref[k, 0:HALO * NB, :] = utb_ref[k, TT * NB:(TT + HALO) * NB, :]


def _const_spec(shape):
    zeros = (0,) * len(shape)
    return pl.BlockSpec(shape, lambda bi, ti: zeros, pipeline_mode=pl.Buffered(1))


def _layer_call(x, kbd, vbd, g_pre, w_in, wpool_bd, pscale, inv_w, inv_c0, wtb, wc, a_mat, d_skip, w_glu, w_out, g_post,
                *, pool_w, ssm_w, att_w, n_mem):
    bsz, seq, d = x.shape
    assert bsz % NB == 0 and seq % TT == 0 and TT % TB == 0 and TT % TC == 0 and TB == HALO
    assert pool_w % LANES == 0 and ssm_w % LANES == 0 and att_w % LANES == 0
    n_pool, n_ssm = pool_w // LANES, ssm_w // LANES
    n_slab = n_pool + n_ssm
    n_state = a_mat.shape[-1]
    m_rows = NB * TT
    kern = functools.partial(_layer_kernel, d=d, pool_w=pool_w, ssm_w=ssm_w, att_w=att_w, n_mem=n_mem)
    consts = (g_pre, w_in, wpool_bd, pscale, inv_w, inv_c0, wtb, wc, a_mat, d_skip, w_glu, w_out, g_post)
    return pl.pallas_call(
        kern,
        grid=(bsz // NB, seq // TT),
        in_specs=[
            pl.BlockSpec((NB, TT, d), lambda bi, ti: (bi, ti, 0)),
            pl.BlockSpec((NB,) + kbd.shape[1:], lambda bi, ti: (bi, 0, 0), pipeline_mode=pl.Buffered(1)),
            pl.BlockSpec((NB,) + vbd.shape[1:], lambda bi, ti: (bi, 0, 0), pipeline_mode=pl.Buffered(1)),
        ] + [_const_spec(c.shape) for c in consts],
        out_specs=pl.BlockSpec((NB, TT, d), lambda bi, ti: (bi, ti, 0)),
        out_shape=jax.ShapeDtypeStruct(x.shape, x.dtype),
        scratch_shapes=[
            pltpu.VMEM((n_slab, (HALO + TT) * NB, LANES), F32),
            pltpu.VMEM((n_slab, TT * NB, LANES), F32),
            pltpu.VMEM((m_rows, d), F32),
            pltpu.VMEM((m_rows, att_w), BF16),
            pltpu.VMEM((m_rows, d), BF16),
            pltpu.VMEM((m_rows, pool_w), BF16),
            pltpu.VMEM((n_ssm, m_rows // TC, n_state), F32),
            pltpu.VMEM((n_ssm, m_rows // TC, n_state), F32),
            pltpu.VMEM((n_ssm, NB, n_state), F32),
            pltpu.VMEM((m_rows, MEM_HEADS * n_mem), F32),
            pltpu.VMEM(w_in.shape, BF16),
            pltpu.VMEM(w_out.shape, BF16),
            pltpu.VMEM((NB,) + kbd.shape[1:], BF16),
        ],
        compiler_params=pltpu.CompilerParams(
            dimension_semantics=("arbitrary", "arbitrary"),
            vmem_limit_bytes=VMEM_LIMIT_BYTES,
        ),
        name="layer_call",
    )(x, kbd, vbd, *consts)


def _ssm_matrices(a_re, a_im, log_dt, b_re, b_im, c_re, c_im):
    n_grp, n_st = a_re.shape
    c = b_re.shape[-1]
    gpb = LANES // c
    nj = n_grp // gpb
    lam = lax.complex(a_re, a_im)
    dt = jnp.exp(log_dt)[:, None]
    ks = jnp.arange(TC + 1, dtype=F32)
    p = jnp.exp((lam * dt)[None] * ks[:, None, None])
    lam_bar = jnp.exp(lam * dt)
    b_bar = ((lam_bar - 1.0) / lam)[..., None] * lax.complex(b_re, b_im)
    c_mat = lax.complex(c_re, c_im)
    eye = jnp.eye(gpb, dtype=F32)
    hi = lax.Precision.HIGHEST
    kern = jnp.einsum('gon,kgn,gni->kgoi', c_mat, p[:TC], b_bar, precision=hi).real
    lag = jnp.arange(TC)[None, :] - jnp.arange(TC)[:, None]
    kst = jnp.where((lag >= 0)[:, :, None, None, None], kern[jnp.clip(lag, 0, TC - 1)], 0.0)
    kst = kst.reshape(TC, TC, nj, gpb, c, c)
    wt = jnp.einsum('stjgoi,gh->jsgitho', kst, eye, precision=hi).reshape(nj, TC * LANES, TC * LANES)
    pb = p[TC - 1 - jnp.arange(TC)][..., None] * b_bar[None]
    pb = pb.reshape(TC, nj, gpb, n_st, c)
    pb = jnp.stack([pb.real, pb.imag], 0)
    wb = jnp.einsum('psjgni,gh->jsgiphn', pb, eye, precision=hi).reshape(nj, TC * LANES, 2 * gpb * n_st)
    cp = c_mat[None] * p[1:TC + 1][:, :, None, :]
    cp = cp.reshape(TC, nj, gpb, c, n_st)
    cp = jnp.stack([cp.real, -cp.imag], 0)
    wc = jnp.einsum('psjgon,gh->jpgnsho', cp, eye, precision=hi).reshape(nj, 2 * gpb * n_st, TC * LANES)
    a_chunk = p[TC].reshape(nj, gpb * n_st)
    a_mat = jnp.concatenate([a_chunk.real, a_chunk.imag], axis=-1)
    a_mat = jnp.broadcast_to(a_mat[:, None, :], (nj, NB, a_mat.shape[-1]))
    wtb = jnp.concatenate([wt, wb], axis=-1).astype(BF16)
    return wtb, wc.astype(BF16), a_mat


def _pool_constants(w_pool, pool_w):
    n_grp, gw, _ = w_pool.shape
    wbd = jnp.zeros((pool_w, pool_w), F32)
    for gi in range(n_grp):
        wbd = wbd.at[gi * gw:(gi + 1) * gw, gi * gw:(gi + 1) * gw].set(w_pool[gi])
    win = jnp.repeat(jnp.asarray(POOL_WINDOWS, F32), gw)
    inv_w = (1.0 / win)[None, :]
    pos = jnp.repeat(jnp.arange(1, HALO + 1, dtype=F32), NB)[:, None]
    inv_c0 = 1.0 / jnp.minimum(pos, win[None, :])
    return wbd.astype(BF16), inv_w, inv_c0


def _layer(x, mem, g_pre, w_in, w_pool, pool_scale, a_re, a_im, log_dt, b_re, b_im, c_re, c_im, d_skip, w_glu,
           g_mem, w_kv, w_out, g_post):
    d = x.shape[-1]
    n_mem = mem.shape[1]
    pool_w = w_pool.shape[0] * w_pool.shape[1]
    ssm_w = d_skip.shape[0]
    att_w = w_kv.shape[1] // 2
    assert a_re.shape[0] * SSM_GROUP == ssm_w and b_re.shape[-1] == SSM_GROUP
    assert pool_w + ssm_w + att_w == d and w_pool.shape[0] == len(POOL_WINDOWS)

    kbd, vbd = _kv_call(mem, g_mem[None, :], w_kv[:, :att_w].T.astype(BF16), w_kv[:, att_w:].astype(BF16))
    wtb, wc, a_mat = _ssm_matrices(a_re, a_im, log_dt, b_re, b_im, c_re, c_im)
    wpool_bd, inv_w, inv_c0 = _pool_constants(w_pool, pool_w)
    return _layer_call(
        x, kbd, vbd, g_pre[None, :], w_in.astype(BF16), wpool_bd, pool_scale[None, :], inv_w, inv_c0,
        wtb, wc, a_mat, d_skip[None, :], w_glu.astype(BF16), w_out.astype(BF16), g_post[None, :],
        pool_w=pool_w, ssm_w=ssm_w, att_w=att_w, n_mem=n_mem)


def kernel(x, mem, g_pre, w_in, w_pool, pool_scale, a_re, a_im, log_dt, b_re, b_im, c_re, c_im, d_skip, w_glu, g_mem, w_kv, w_out, g_post):
    for i in range(g_pre.shape[0]):
        x = _layer(x, mem, g_pre[i], w_in[i], w_pool[i], pool_scale[i], a_re[i], a_im[i], log_dt[i], b_re[i], b_im[i],
                   c_re[i], c_im[i], d_skip[i], w_glu[i], g_mem[i], w_kv[i], w_out[i], g_post[i])
    return x
```

```python
import functools

import jax
import jax.numpy as jnp
from jax import lax
from jax.experimental import pallas as pl
from jax.experimental.pallas import tpu as pltpu

LANES = 128
NB = 8
TT = 64
TC = 4
BC = 4
TB = 16
HALO = 16
POOL_WINDOWS = (2, 4, 8, 16)
SSM_GROUP = 16
MEM_HEADS = 4
EPS = 1e-6
VMEM_LIMIT_BYTES = 56 * 1024 * 1024

F32 = jnp.float32
BF16 = jnp.bfloat16


def _sigmoid(v):
    return 0.5 * jnp.tanh(0.5 * v) + 0.5


def _kv_kernel(mem_ref, gmem_ref, wkt_ref, wv_ref, kbd_ref, vbd_ref, *, att_w, n_mem):
    m = mem_ref[0]
    ms = jnp.mean(m * m, axis=-1, keepdims=True)
    mn = ((m * lax.rsqrt(ms + EPS)) * gmem_ref[...]).astype(BF16)
    kt = lax.dot_general(wkt_ref[...], mn, (((1,), (1,)), ((), ())), preferred_element_type=F32)
    v = jnp.dot(mn, wv_ref[...], preferred_element_type=F32)
    hd = att_w // MEM_HEADS
    kt = kt * (hd ** -0.5)
    row_head = lax.broadcasted_iota(jnp.int32, (att_w, n_mem), 0) // hd
    col_head = lax.broadcasted_iota(jnp.int32, (n_mem, att_w), 1) // hd
    for hh in range(MEM_HEADS):
        kbd_ref[0, :, hh * n_mem:(hh + 1) * n_mem] = jnp.where(row_head == hh, kt, 0.0).astype(BF16)
        vbd_ref[0, hh * n_mem:(hh + 1) * n_mem, :] = jnp.where(col_head == hh, v, 0.0).astype(BF16)


def _kv_call(mem, g_mem, wkt, wv):
    bsz, n_mem, d = mem.shape
    att_w = wv.shape[1]
    kern = functools.partial(_kv_kernel, att_w=att_w, n_mem=n_mem)
    return pl.pallas_call(
        kern,
        grid=(bsz,),
        in_specs=[
            pl.BlockSpec((1, n_mem, d), lambda b: (b, 0, 0)),
            pl.BlockSpec((1, d), lambda b: (0, 0)),
            pl.BlockSpec((att_w, d), lambda b: (0, 0)),
            pl.BlockSpec((d, att_w), lambda b: (0, 0)),
        ],
        out_specs=[
            pl.BlockSpec((1, att_w, MEM_HEADS * n_mem), lambda b: (b, 0, 0)),
            pl.BlockSpec((1, MEM_HEADS * n_mem, att_w), lambda b: (b, 0, 0)),
        ],
        out_shape=[
            jax.ShapeDtypeStruct((bsz, att_w, MEM_HEADS * n_mem), BF16),
            jax.ShapeDtypeStruct((bsz, MEM_HEADS * n_mem, att_w), BF16),
        ],
        compiler_params=pltpu.CompilerParams(dimension_semantics=("arbitrary",)),
        name="kv_call",
    )(mem, g_mem, wkt, wv)


def _layer_kernel(x_ref, kbd_ref, vbd_ref, gpre_ref, win_ref, wpool_ref, pscale_ref, invw_ref, invc0_ref,
                  wtb_ref, wc_ref, a_ref, dskip_ref, wglu_ref, wout_ref, gpost_ref,
                  out_ref,
                  utb_ref, ytb_ref, gate_ref, q_ref, ycat_ref, pd_ref, g_ref, hin_ref, hc_ref,
                  s_ref, win_s, wout_s, kbd_s,
                  *, d, pool_w, ssm_w, att_w, n_mem):
    bi = pl.program_id(0)
    ti = pl.program_id(1)
    n_pool = pool_w // LANES
    n_ssm = ssm_w // LANES
    n_slab = n_pool + n_ssm
    mix_w = pool_w + ssm_w
    n_chunk = TT // TC
    st = hc_ref.shape[-1] // 2
    hd = att_w // MEM_HEADS

    @pl.when((bi == 0) & (ti == 0))
    def _():
        win_s[...] = win_ref[...]
        wout_s[...] = wout_ref[...]

    @pl.when(ti == 0)
    def _():
        kbd_s[...] = kbd_ref[...]
        utb_ref[0:n_pool, 0:HALO * NB, :] = jnp.zeros((n_pool, HALO * NB, LANES), F32)
        hc_ref[...] = jnp.zeros(hc_ref.shape, F32)

    for c in range(NB // BC):
        crow = slice(c * BC * TT, (c + 1) * BC * TT)
        x = x_ref[c * BC:(c + 1) * BC].reshape(BC * TT, d)
        ms = jnp.mean(x * x, axis=-1, keepdims=True)
        h = ((x * lax.rsqrt(ms + EPS)) * gpre_ref[...]).astype(BF16)
        val = jnp.dot(h, win_s[:, 0:mix_w], preferred_element_type=F32)
        for bb in range(BC):
            for k in range(n_slab):
                utb_ref[k, pl.ds(HALO * NB + c * BC + bb, TT, stride=NB), :] = (
                    val[bb * TT:(bb + 1) * TT, k * LANES:(k + 1) * LANES])
        q_ref[crow, :] = jnp.dot(h, win_s[:, mix_w:mix_w + att_w], preferred_element_type=F32).astype(BF16)
        gate = jnp.dot(h, win_s[:, mix_w + att_w:], preferred_element_type=F32)
        gate_ref[crow, :] = gate * _sigmoid(gate)

    u_tb, y_intra = [], []
    for j in range(n_ssm):
        u = utb_ref[n_pool + j, HALO * NB:(HALO + TT) * NB, :]
        u4 = u.reshape(n_chunk, TC, NB, LANES)
        up = jnp.concatenate([u4[:, s].reshape(n_chunk * NB, LANES) for s in range(TC)], axis=1).astype(BF16)
        yg = jnp.dot(up, wtb_ref[j], preferred_element_type=F32)
        y_intra.append(yg[:, 0:TC * LANES])
        g_ref[j] = yg[:, TC * LANES:]
        u_tb.append(u)

    for b in range(NB):
        rows = slice(b * TT, (b + 1) * TT)
        s_ref[rows, :] = jnp.dot(q_ref[rows, :], kbd_s[b], preferred_element_type=F32)

    gw = pool_w // len(POOL_WINDOWS)
    lane = lax.broadcasted_iota(jnp.int32, (TB * NB, LANES), 1)
    for k in range(n_pool):
        w_lo = POOL_WINDOWS[(k * LANES) // gw]
        w_hi = POOL_WINDOWS[min(((k + 1) * LANES - 1) // gw, len(POOL_WINDOWS) - 1)]
        split = ((k * LANES) // gw + 1) * gw - k * LANES
        for i in range(TT // TB):
            base = (HALO + i * TB) * NB
            ext = utb_ref[k, base - (HALO - 1) * NB: base + TB * NB, :]
            sums = {1: ext}
            w = 1
            while w < w_hi:
                prev = sums[w]
                sums[2 * w] = prev[w * NB:] + prev[:-w * NB]
                w *= 2
            cur = ext[(HALO - 1) * NB:]
            lo = sums[w_lo][(HALO - w_lo) * NB:]
            hi = sums[w_hi][(HALO - w_hi) * NB:]
            ssum = jnp.where(lane < split, lo, hi)
            scale = invw_ref[:, k * LANES:(k + 1) * LANES]
            if i == 0:
                scale = jnp.where(ti == 0, invc0_ref[:, k * LANES:(k + 1) * LANES], scale)
            pd_ref[i * TB * NB:(i + 1) * TB * NB, k * LANES:(k + 1) * LANES] = (ssum * scale - cur).astype(BF16)
    ypool = jnp.dot(pd_ref[...], wpool_ref[...], preferred_element_type=F32) * pscale_ref[...]
    for k in range(n_pool):
        ytb_ref[k] = ypool[:, k * LANES:(k + 1) * LANES]

    a_re = [a_ref[j, :, 0:st] for j in range(n_ssm)]
    a_im = [a_ref[j, :, st:] for j in range(n_ssm)]
    h_re = [hc_ref[j, :, 0:st] for j in range(n_ssm)]
    h_im = [hc_ref[j, :, st:] for j in range(n_ssm)]
    for r in range(n_chunk):
        rows = slice(r * NB, (r + 1) * NB)
        for j in range(n_ssm):
            hin_ref[j, rows, 0:st] = h_re[j]
            hin_ref[j, rows, st:] = h_im[j]
            g_re = g_ref[j, rows, 0:st]
            g_im = g_ref[j, rows, st:]
            h_re[j], h_im[j] = (a_re[j] * h_re[j] - a_im[j] * h_im[j] + g_re,
                                a_re[j] * h_im[j] + a_im[j] * h_re[j] + g_im)
    for j in range(n_ssm):
        hc_ref[j, :, 0:st] = h_re[j]
        hc_ref[j, :, st:] = h_im[j]

    y_act = []
    for j in range(n_ssm):
        y_lin = y_intra[j] + jnp.dot(hin_ref[j].astype(BF16), wc_ref[j], preferred_element_type=F32)
        y_tb = jnp.stack([y_lin[:, s * LANES:(s + 1) * LANES].reshape(n_chunk, NB, LANES) for s in range(TC)],
                         axis=1).reshape(TT * NB, LANES)
        y = y_tb + dskip_ref[:, j * LANES:(j + 1) * LANES] * u_tb[j]
        y_act.append(jax.nn.gelu(y).astype(BF16))

    lane_t = lax.broadcasted_iota(jnp.int32, (TT, LANES), 1)
    heads_per_vreg = LANES // hd
    for b in range(NB):
        rows = slice(b * TT, (b + 1) * TT)
        es, inv = [], []
        for hh in range(MEM_HEADS):
            sh = s_ref[rows, hh * n_mem:(hh + 1) * n_mem]
            e = jnp.exp(sh - jnp.max(sh, axis=-1, keepdims=True))
            inv.append(1.0 / jnp.sum(e, axis=-1, keepdims=True))
            es.append(e.astype(BF16))
        o = jnp.dot(jnp.concatenate(es, axis=1), vbd_ref[b], preferred_element_type=F32)
        norm = []
        for c in range(att_w // LANES):
            sc = inv[c * heads_per_vreg]
            for e_i in range(1, heads_per_vreg):
                sc = jnp.where(lane_t < e_i * hd, sc, inv[c * heads_per_vreg + e_i])
            norm.append(sc)
        o = o * jnp.concatenate(norm, axis=1)
        ycat_ref[rows, mix_w:] = (o * gate_ref[rows, mix_w:]).astype(BF16)

    z = jnp.dot(jnp.concatenate(y_act, axis=1), wglu_ref[...], preferred_element_type=F32)
    y_ssm = z[:, 0:ssm_w] * _sigmoid(z[:, ssm_w:])
    for j in range(n_ssm):
        ytb_ref[n_pool + j] = y_ssm[:, j * LANES:(j + 1) * LANES]

    for c in range(NB // BC):
        crow = slice(c * BC * TT, (c + 1) * BC * TT)
        for bb in range(BC):
            b = c * BC + bb
            rows = slice(b * TT, (b + 1) * TT)
            for k in range(n_slab):
                cols = slice(k * LANES, (k + 1) * LANES)
                yk = ytb_ref[k, pl.ds(b, TT, stride=NB), :]
                ycat_ref[rows, cols] = (yk * gate_ref[rows, cols]).astype(BF16)
        out = jnp.dot(ycat_ref[crow, :], wout_s[...], preferred_element_type=F32)
        ms2 = jnp.mean(out * out, axis=-1, keepdims=True)
        res = x_ref[c * BC:(c + 1) * BC].reshape(BC * TT, d) + (out * lax.rsqrt(ms2 + EPS)) * gpost_ref[...]
        out_ref[c * BC:(c + 1) * BC] = res.reshape(BC, TT, d)

    for k in range(n_pool):
        utb_ref[k, 0:HALO * NB, :] = utb_ref[k, TT * NB:(TT + HALO) * NB, :]


def _const_spec(shape):
    zeros = (0,) * len(shape)
    return pl.BlockSpec(shape, lambda bi, ti: zeros, pipeline_mode=pl.Buffered(1))


def _layer_call(x, kbd, vbd, g_pre, w_in, wpool_bd, pscale, inv_w, inv_c0, wtb, wc, a_mat, d_skip, w_glu, w_out, g_post,
                *, pool_w, ssm_w, att_w, n_mem):
    bsz, seq, d = x.shape
    assert bsz % NB == 0 and seq % TT == 0 and TT % TB == 0 and TT % TC == 0 and TB == HALO
    assert pool_w % LANES == 0 and ssm_w % LANES == 0 and att_w % LANES == 0
    n_pool, n_ssm = pool_w // LANES, ssm_w // LANES
    n_slab = n_pool + n_ssm
    n_state = a_mat.shape[-1]
    m_rows = NB * TT
    kern = functools.partial(_layer_kernel, d=d, pool_w=pool_w, ssm_w=ssm_w, att_w=att_w, n_mem=n_mem)
    consts = (g_pre, w_in, wpool_bd, pscale, inv_w, inv_c0, wtb, wc, a_mat, d_skip, w_glu, w_out, g_post)
    return pl.pallas_call(
        kern,
        grid=(bsz // NB, seq // TT),
        in_specs=[
            pl.BlockSpec((NB, TT, d), lambda bi, ti: (bi, ti, 0)),
            pl.BlockSpec((NB,) + kbd.shape[1:], lambda bi, ti: (bi, 0, 0), pipeline_mode=pl.Buffered(1)),
            pl.BlockSpec((NB,) + vbd.shape[1:], lambda bi, ti: (bi, 0, 0), pipeline_mode=pl.Buffered(1)),
        ] + [_const_spec(c.shape) for c in consts],
        out_specs=pl.BlockSpec((NB, TT, d), lambda bi, ti: (bi, ti, 0)),
        out_shape=jax.ShapeDtypeStruct(x.shape, x.dtype),
        scratch_shapes=[
            pltpu.VMEM((n_slab, (HALO + TT) * NB, LANES), F32),
            pltpu.VMEM((n_slab, TT * NB, LANES), F32),
            pltpu.VMEM((m_rows, d), F32),
            pltpu.VMEM((m_rows, att_w), BF16),
            pltpu.VMEM((m_rows, d), BF16),
            pltpu.VMEM((m_rows, pool_w), BF16),
            pltpu.VMEM((n_ssm, m_rows // TC, n_state), F32),
            pltpu.VMEM((n_ssm, m_rows // TC, n_state), F32),
            pltpu.VMEM((n_ssm, NB, n_state), F32),
            pltpu.VMEM((m_rows, MEM_HEADS * n_mem), F32),
            pltpu.VMEM(w_in.shape, BF16),
            pltpu.VMEM(w_out.shape, BF16),
            pltpu.VMEM((NB,) + kbd.shape[1:], BF16),
        ],
        compiler_params=pltpu.CompilerParams(
            dimension_semantics=("arbitrary", "arbitrary"),
            vmem_limit_bytes=VMEM_LIMIT_BYTES,
        ),
        name="layer_call",
    )(x, kbd, vbd, *consts)


def _ssm_matrices(a_re, a_im, log_dt, b_re, b_im, c_re, c_im):
    n_grp, n_st = a_re.shape
    c = b_re.shape[-1]
    gpb = LANES // c
    nj = n_grp // gpb
    lam = lax.complex(a_re, a_im)
    dt = jnp.exp(log_dt)[:, None]
    ks = jnp.arange(TC + 1, dtype=F32)
    p = jnp.exp((lam * dt)[None] * ks[:, None, None])
    lam_bar = jnp.exp(lam * dt)
    b_bar = ((lam_bar - 1.0) / lam)[..., None] * lax.complex(b_re, b_im)
    c_mat = lax.complex(c_re, c_im)
    eye = jnp.eye(gpb, dtype=F32)
    hi = lax.Precision.HIGHEST
    kern = jnp.einsum('gon,kgn,gni->kgoi', c_mat, p[:TC], b_bar, precision=hi).real
    lag = jnp.arange(TC)[None, :] - jnp.arange(TC)[:, None]
    kst = jnp.where((lag >= 0)[:, :, None, None, None], kern[jnp.clip(lag, 0, TC - 1)], 0.0)
    kst = kst.reshape(TC, TC, nj, gpb, c, c)
    wt = jnp.einsum('stjgoi,gh->jsgitho', kst, eye, precision=hi).reshape(nj, TC * LANES, TC * LANES)
    pb = p[TC - 1 - jnp.arange(TC)][..., None] * b_bar[None]
    pb = pb.reshape(TC, nj, gpb, n_st, c)
    pb = jnp.stack([pb.real, pb.imag], 0)
    wb = jnp.einsum('psjgni,gh->jsgiphn', pb, eye, precision=hi).reshape(nj, TC * LANES, 2 * gpb * n_st)
    cp = c_mat[None] * p[1:TC + 1][:, :, None, :]
    cp = cp.reshape(TC, nj, gpb, c, n_st)
    cp = jnp.stack([cp.real, -cp.imag], 0)
    wc = jnp.einsum('psjgon,gh->jpgnsho', cp, eye, precision=hi).reshape(nj, 2 * gpb * n_st, TC * LANES)
    a_chunk = p[TC].reshape(nj, gpb * n_st)
    a_mat = jnp.concatenate([a_chunk.real, a_chunk.imag], axis=-1)
    a_mat = jnp.broadcast_to(a_mat[:, None, :], (nj, NB, a_mat.shape[-1]))
    wtb = jnp.concatenate([wt, wb], axis=-1).astype(BF16)
    return wtb, wc.astype(BF16), a_mat


def _pool_constants(w_pool, pool_w):
    n_grp, gw, _ = w_pool.shape
    wbd = jnp.zeros((pool_w, pool_w), F32)
    for gi in range(n_grp):
        wbd = wbd.at[gi * gw:(gi + 1) * gw, gi * gw:(gi + 1) * gw].set(w_pool[gi])
    win = jnp.repeat(jnp.asarray(POOL_WINDOWS, F32), gw)
    inv_w = (1.0 / win)[None, :]
    pos = jnp.repeat(jnp.arange(1, HALO + 1, dtype=F32), NB)[:, None]
    inv_c0 = 1.0 / jnp.minimum(pos, win[None, :])
    return wbd.astype(BF16), inv_w, inv_c0


def _layer(x, mem, g_pre, w_in, w_pool, pool_scale, a_re, a_im, log_dt, b_re, b_im, c_re, c_im, d_skip, w_glu,
           g_mem, w_kv, w_out, g_post):
    d = x.shape[-1]
    n_mem = mem.shape[1]
    pool_w = w_pool.shape[0] * w_pool.shape[1]
    ssm_w = d_skip.shape[0]
    att_w = w_kv.shape[1] // 2
    assert a_re.shape[0] * SSM_GROUP == ssm_w and b_re.shape[-1] == SSM_GROUP
    assert pool_w + ssm_w + att_w == d and w_pool.shape[0] == len(POOL_WINDOWS)

    kbd, vbd = _kv_call(mem, g_mem[None, :], w_kv[:, :att_w].T.astype(BF16), w_kv[:, att_w:].astype(BF16))
    wtb, wc, a_mat = _ssm_matrices(a_re, a_im, log_dt, b_re, b_im, c_re, c_im)
    wpool_bd, inv_w, inv_c0 = _pool_constants(w_pool, pool_w)
    return _layer_call(
        x, kbd, vbd, g_pre[None, :], w_in.astype(BF16), wpool_bd, pool_scale[None, :], inv_w, inv_c0,
        wtb, wc, a_mat, d_skip[None, :], w_glu.astype(BF16), w_out.astype(BF16), g_post[None, :],
        pool_w=pool_w, ssm_w=ssm_w, att_w=att_w, n_mem=n_mem)


def kernel(x, mem, g_pre, w_in, w_pool, pool_scale, a_re, a_im, log_dt, b_re, b_im, c_re, c_im, d_skip, w_glu, g_mem, w_kv, w_out, g_post):
    for i in range(g_pre.shape[0]):
        x = _layer(x, mem, g_pre[i], w_in[i], w_pool[i], pool_scale[i], a_re[i], a_im[i], log_dt[i], b_re[i], b_im[i],
                   c_re[i], c_im[i], d_skip[i], w_glu[i], g_mem[i], w_kv[i], w_out[i], g_post[i])
    return x
```

```python
import functools

import jax
import jax.numpy as jnp
from jax import lax
from jax.experimental import pallas as pl
from jax.experimental.pallas import tpu as pltpu

LANES = 128
NB = 8
TT = 64
TC = 4
BC = 4
TB = 16
KV_MB = 4
HALO = 16
POOL_WINDOWS = (2, 4, 8, 16)
SSM_GROUP = 16
MEM_HEADS = 4
EPS = 1e-6
VMEM_LIMIT_BYTES = 56 * 1024 * 1024

F32 = jnp.float32
BF16 = jnp.bfloat16


def _sigmoid(v):
    return 0.5 * jnp.tanh(0.5 * v) + 0.5


def _kv_kernel(mem_ref, gmem_ref, wkt_ref, wv_ref, kbd_ref, vbd_ref, *, att_w, n_mem):
    mb, _, d = mem_ref.shape
    m = mem_ref[...].reshape(mb * n_mem, d)
    ms = jnp.mean(m * m, axis=-1, keepdims=True)
    mn = ((m * lax.rsqrt(ms + EPS)) * gmem_ref[...]).astype(BF16)
    kt = lax.dot_general(wkt_ref[...], mn, (((1,), (1,)), ((), ())), preferred_element_type=F32)
    v = jnp.dot(mn, wv_ref[...], preferred_element_type=F32)
    hd = att_w // MEM_HEADS
    kt = kt * (hd ** -0.5)
    row_head = lax.broadcasted_iota(jnp.int32, (att_w, n_mem), 0) // hd
    col_head = lax.broadcasted_iota(jnp.int32, (n_mem, att_w), 1) // hd
    for i in range(mb):
        kt_i = kt[:, i * n_mem:(i + 1) * n_mem]
        v_i = v[i * n_mem:(i + 1) * n_mem, :]
        for hh in range(MEM_HEADS):
            kbd_ref[i, :, hh * n_mem:(hh + 1) * n_mem] = jnp.where(row_head == hh, kt_i, 0.0).astype(BF16)
            vbd_ref[i, hh * n_mem:(hh + 1) * n_mem, :] = jnp.where(col_head == hh, v_i, 0.0).astype(BF16)


def _kv_call(mem, g_mem, wkt, wv):
    bsz, n_mem, d = mem.shape
    att_w = wv.shape[1]
    assert bsz % KV_MB == 0
    kern = functools.partial(_kv_kernel, att_w=att_w, n_mem=n_mem)
    return pl.pallas_call(
        kern,
        grid=(bsz // KV_MB,),
        in_specs=[
            pl.BlockSpec((KV_MB, n_mem, d), lambda b: (b, 0, 0)),
            pl.BlockSpec((1, d), lambda b: (0, 0)),
            pl.BlockSpec((att_w, d), lambda b: (0, 0)),
            pl.BlockSpec((d, att_w), lambda b: (0, 0)),
        ],
        out_specs=[
            pl.BlockSpec((KV_MB, att_w, MEM_HEADS * n_mem), lambda b: (b, 0, 0)),
            pl.BlockSpec((KV_MB, MEM_HEADS * n_mem, att_w), lambda b: (b, 0, 0)),
        ],
        out_shape=[
            jax.ShapeDtypeStruct((bsz, att_w, MEM_HEADS * n_mem), BF16),
            jax.ShapeDtypeStruct((bsz, MEM_HEADS * n_mem, att_w), BF16),
        ],
        compiler_params=pltpu.CompilerParams(dimension_semantics=("arbitrary",)),
        name="kv_call",
    )(mem, g_mem, wkt, wv)


def _layer_kernel(x_ref, kbd_ref, vbd_ref, gpre_ref, win_ref, wpool_ref, pscale_ref, invw_ref, invc0_ref,
                  wtb_ref, wc_ref, a_ref, dskip_ref, wglu_ref, wout_ref, gpost_ref,
                  out_ref,
                  utb_ref, ytb_ref, gate_ref, q_ref, ycat_ref, pd_ref, g_ref, hin_ref, hc_ref,
                  s_ref, win_s, wout_s, kbd_s,
                  *, d, pool_w, ssm_w, att_w, n_mem):
    bi = pl.program_id(0)
    ti = pl.program_id(1)
    n_pool = pool_w // LANES
    n_ssm = ssm_w // LANES
    n_slab = n_pool + n_ssm
    mix_w = pool_w + ssm_w
    n_chunk = TT // TC
    st = hc_ref.shape[-1] // 2
    hd = att_w // MEM_HEADS

    @pl.when((bi == 0) & (ti == 0))
    def _():
        pltpu.sync_copy(win_ref, win_s)
        pltpu.sync_copy(wout_ref, wout_s)

    @pl.when(ti == 0)
    def _():
        kbd_s[...] = kbd_ref[...]
        utb_ref[0:n_pool, 0:HALO * NB, :] = jnp.zeros((n_pool, HALO * NB, LANES), F32)
        hc_ref[...] = jnp.zeros(hc_ref.shape, F32)

    for c in range(NB // BC):
        crow = slice(c * BC * TT, (c + 1) * BC * TT)
        x = x_ref[c * BC:(c + 1) * BC].reshape(BC * TT, d)
        ms = jnp.mean(x * x, axis=-1, keepdims=True)
        h = ((x * lax.rsqrt(ms + EPS)) * gpre_ref[...]).astype(BF16)
        val = jnp.dot(h, win_s[:, 0:mix_w], preferred_element_type=F32)
        for bb in range(BC):
            for k in range(n_slab):
                utb_ref[k, pl.ds(HALO * NB + c * BC + bb, TT, stride=NB), :] = (
                    val[bb * TT:(bb + 1) * TT, k * LANES:(k + 1) * LANES])
        q_ref[crow, :] = jnp.dot(h, win_s[:, mix_w:mix_w + att_w], preferred_element_type=F32).astype(BF16)
        gate = jnp.dot(h, win_s[:, mix_w + att_w:], preferred_element_type=F32)
        gate_ref[crow, :] = gate * _sigmoid(gate)

    u_tb, y_intra = [], []
    for j in range(n_ssm):
        u = utb_ref[n_pool + j, HALO * NB:(HALO + TT) * NB, :]
        u4 = u.reshape(n_chunk, TC, NB, LANES)
        up = jnp.concatenate([u4[:, s].reshape(n_chunk * NB, LANES) for s in range(TC)], axis=1).astype(BF16)
        yg = jnp.dot(up, wtb_ref[j], preferred_element_type=F32)
        y_intra.append(yg[:, 0:TC * LANES])
        g_ref[j] = yg[:, TC * LANES:]
        u_tb.append(u)

    for b in range(NB):
        rows = slice(b * TT, (b + 1) * TT)
        s_ref[rows, :] = jnp.dot(q_ref[rows, :], kbd_s[b], preferred_element_type=F32)

    gw = pool_w // len(POOL_WINDOWS)
    lane = lax.broadcasted_iota(jnp.int32, (TB * NB, LANES), 1)
    for k in range(n_pool):
        w_lo = POOL_WINDOWS[(k * LANES) // gw]
        w_hi = POOL_WINDOWS[min(((k + 1) * LANES - 1) // gw, len(POOL_WINDOWS) - 1)]
        split = ((k * LANES) // gw + 1) * gw - k * LANES
        for i in range(TT // TB):
            base = (HALO + i * TB) * NB
            ext = utb_ref[k, base - (HALO - 1) * NB: base + TB * NB, :]
            sums = {1: ext}
            w = 1
            while w < w_hi:
                prev = sums[w]
                sums[2 * w] = prev[w * NB:] + prev[:-w * NB]
                w *= 2
            cur = ext[(HALO - 1) * NB:]
            lo = sums[w_lo][(HALO - w_lo) * NB:]
            hi = sums[w_hi][(HALO - w_hi) * NB:]
            ssum = jnp.where(lane < split, lo, hi)
            scale = invw_ref[:, k * LANES:(k + 1) * LANES]
            if i == 0:
                scale = jnp.where(ti == 0, invc0_ref[:, k * LANES:(k + 1) * LANES], scale)
            pd_ref[i * TB * NB:(i + 1) * TB * NB, k * LANES:(k + 1) * LANES] = (ssum * scale - cur).astype(BF16)
    ypool = jnp.dot(pd_ref[...], wpool_ref[...], preferred_element_type=F32) * pscale_ref[...]
    for k in range(n_pool):
        ytb_ref[k] = ypool[:, k * LANES:(k + 1) * LANES]

    a_re = [a_ref[j, :, 0:st] for j in range(n_ssm)]
    a_im = [a_ref[j, :, st:] for j in range(n_ssm)]
    h_re = [hc_ref[j, :, 0:st] for j in range(n_ssm)]
    h_im = [hc_ref[j, :, st:] for j in range(n_ssm)]
    for r in range(n_chunk):
        rows = slice(r * NB, (r + 1) * NB)
        for j in range(n_ssm):
            hin_ref[j, rows, 0:st] = h_re[j]
            hin_ref[j, rows, st:] = h_im[j]
            g_re = g_ref[j, rows, 0:st]
            g_im = g_ref[j, rows, st:]
            h_re[j], h_im[j] = (a_re[j] * h_re[j] - a_im[j] * h_im[j] + g_re,
                                a_re[j] * h_im[j] + a_im[j] * h_re[j] + g_im)
    for j in range(n_ssm):
        hc_ref[j, :, 0:st] = h_re[j]
        hc_ref[j, :, st:] = h_im[j]

    y_act = []
    for j in range(n_ssm):
        y_lin = y_intra[j] + jnp.dot(hin_ref[j].astype(BF16), wc_ref[j], preferred_element_type=F32)
        y_tb = jnp.stack([y_lin[:, s * LANES:(s + 1) * LANES].reshape(n_chunk, NB, LANES) for s in range(TC)],
                         axis=1).reshape(TT * NB, LANES)
        y = y_tb + dskip_ref[:, j * LANES:(j + 1) * LANES] * u_tb[j]
        y_act.append(jax.nn.gelu(y).astype(BF16))

    lane_t = lax.broadcasted_iota(jnp.int32, (TT, LANES), 1)
    heads_per_vreg = LANES // hd
    for b in range(NB):
        rows = slice(b * TT, (b + 1) * TT)
        es, inv = [], []
        for hh in range(MEM_HEADS):
            sh = s_ref[rows, hh * n_mem:(hh + 1) * n_mem]
            e = jnp.exp(sh - jnp.max(sh, axis=-1, keepdims=True))
            inv.append(1.0 / jnp.sum(e, axis=-1, keepdims=True))
            es.append(e.astype(BF16))
        o = jnp.dot(jnp.concatenate(es, axis=1), vbd_ref[b], preferred_element_type=F32)
        norm = []
        for c in range(att_w // LANES):
            sc = inv[c * heads_per_vreg]
            for e_i in range(1, heads_per_vreg):
                sc = jnp.where(lane_t < e_i * hd, sc, inv[c * heads_per_vreg + e_i])
            norm.append(sc)
        o = o * jnp.concatenate(norm, axis=1)
        ycat_ref[rows, mix_w:] = (o * gate_ref[rows, mix_w:]).astype(BF16)

    z = jnp.dot(jnp.concatenate(y_act, axis=1), wglu_ref[...], preferred_element_type=F32)
    y_ssm = z[:, 0:ssm_w] * _sigmoid(z[:, ssm_w:])
    for j in range(n_ssm):
        ytb_ref[n_pool + j] = y_ssm[:, j * LANES:(j + 1) * LANES]

    for c in range(NB // BC):
        crow = slice(c * BC * TT, (c + 1) * BC * TT)
        for bb in range(BC):
            b = c * BC + bb
            rows = slice(b * TT, (b + 1) * TT)
            for k in range(n_slab):
                cols = slice(k * LANES, (k + 1) * LANES)
                yk = ytb_ref[k, pl.ds(b, TT, stride=NB), :]
                ycat_ref[rows, cols] = (yk * gate_ref[rows, cols]).astype(BF16)
        out = jnp.dot(ycat_ref[crow, :], wout_s[...], preferred_element_type=F32)
        ms2 = jnp.mean(out * out, axis=-1, keepdims=True)
        res = x_ref[c * BC:(c + 1) * BC].reshape(BC * TT, d) + (out * lax.rsqrt(ms2 + EPS)) * gpost_ref[...]
        out_ref[c * BC:(c + 1) * BC] = res.reshape(BC, TT, d)

    for k in range(n_pool):
        utb_ref[k, 0:HALO * NB, :] = utb_ref[k, TT * NB:(TT + HALO) * NB, :]


def _const_spec(shape):
    zeros = (0,) * len(shape)
    return pl.BlockSpec(shape, lambda bi, ti: zeros, pipeline_mode=pl.Buffered(1))


def _layer_call(x, kbd, vbd, g_pre, w_in, wpool_bd, pscale, inv_w, inv_c0, wtb, wc, a_mat, d_skip, w_glu, w_out, g_post,
                *, pool_w, ssm_w, att_w, n_mem):
    bsz, seq, d = x.shape
    assert bsz % NB == 0 and seq % TT == 0 and TT % TB == 0 and TT % TC == 0 and TB == HALO
    assert pool_w % LANES == 0 and ssm_w % LANES == 0 and att_w % LANES == 0
    n_pool, n_ssm = pool_w // LANES, ssm_w // LANES
    n_slab = n_pool + n_ssm
    n_state = a_mat.shape[-1]
    m_rows = NB * TT
    kern = functools.partial(_layer_kernel, d=d, pool_w=pool_w, ssm_w=ssm_w, att_w=att_w, n_mem=n_mem)
    consts = (g_pre, w_in, wpool_bd, pscale, inv_w, inv_c0, wtb, wc, a_mat, d_skip, w_glu, w_out, g_post)
    return pl.pallas_call(
        kern,
        grid=(bsz // NB, seq // TT),
        in_specs=[
            pl.BlockSpec((NB, TT, d), lambda bi, ti: (bi, ti, 0)),
            pl.BlockSpec((NB,) + kbd.shape[1:], lambda bi, ti: (bi, 0, 0), pipeline_mode=pl.Buffered(1)),
            pl.BlockSpec((NB,) + vbd.shape[1:], lambda bi, ti: (bi, 0, 0), pipeline_mode=pl.Buffered(1)),
        ] + [pl.BlockSpec(memory_space=pl.ANY) if c is w_in or c is w_out else _const_spec(c.shape) for c in consts],
        out_specs=pl.BlockSpec((NB, TT, d), lambda bi, ti: (bi, ti, 0)),
        out_shape=jax.ShapeDtypeStruct(x.shape, x.dtype),
        scratch_shapes=[
            pltpu.VMEM((n_slab, (HALO + TT) * NB, LANES), F32),
            pltpu.VMEM((n_slab, TT * NB, LANES), F32),
            pltpu.VMEM((m_rows, d), F32),
            pltpu.VMEM((m_rows, att_w), BF16),
            pltpu.VMEM((m_rows, d), BF16),
            pltpu.VMEM((m_rows, pool_w), BF16),
            pltpu.VMEM((n_ssm, m_rows // TC, n_state), F32),
            pltpu.VMEM((n_ssm, m_rows // TC, n_state), F32),
            pltpu.VMEM((n_ssm, NB, n_state), F32),
            pltpu.VMEM((m_rows, MEM_HEADS * n_mem), F32),
            pltpu.VMEM(w_in.shape, BF16),
            pltpu.VMEM(w_out.shape, BF16),
            pltpu.VMEM((NB,) + kbd.shape[1:], BF16),
        ],
        compiler_params=pltpu.CompilerParams(
            dimension_semantics=("arbitrary", "arbitrary"),
            vmem_limit_bytes=VMEM_LIMIT_BYTES,
        ),
        name="layer_call",
    )(x, kbd, vbd, *consts)


def _expand_kernel(xt_ref, xb_ref, xc_ref, wtb_ref, wc_ref, *, c, n_st):
    gpb = LANES // c
    tl = TC * LANES

    def rep(rows, cols, row_blk, col_blk, period):
        r = lax.broadcasted_iota(jnp.int32, (rows, cols), 0)
        q = lax.broadcasted_iota(jnp.int32, (rows, cols), 1)
        hit = (r // row_blk == q // col_blk) & (r % period == q % period)
        return jnp.where(hit, 1.0, 0.0).astype(BF16)

    def same_group(rows, cols, row_div, col_div):
        r = lax.broadcasted_iota(jnp.int32, (rows, cols), 0)
        q = lax.broadcasted_iota(jnp.int32, (rows, cols), 1)
        return (r // row_div) % gpb == (q // col_div) % gpb

    rep_t = rep(TC * c, tl, c, LANES, c)
    rep_b = rep(2 * n_st, 2 * gpb * n_st, n_st, gpb * n_st, n_st)
    wt = jnp.dot(xt_ref[0].astype(BF16), rep_t, preferred_element_type=F32)
    wtb_ref[0, :, 0:tl] = jnp.where(same_group(tl, tl, c, c), wt, 0.0).astype(BF16)
    wb = jnp.dot(xb_ref[0].astype(BF16), rep_b, preferred_element_type=F32)
    wtb_ref[0, :, tl:] = jnp.where(same_group(tl, 2 * gpb * n_st, c, n_st), wb, 0.0).astype(BF16)
    wcx = jnp.dot(xc_ref[0].astype(BF16), rep_t, preferred_element_type=F32)
    wc_ref[0] = jnp.where(same_group(2 * gpb * n_st, tl, n_st, c), wcx, 0.0).astype(BF16)


def _ssm_matrices(a_re, a_im, log_dt, b_re, b_im, c_re, c_im):
    n_grp, n_st = a_re.shape
    c = b_re.shape[-1]
    gpb = LANES // c
    nj = n_grp // gpb
    hi = lax.Precision.HIGHEST
    dt = jnp.exp(log_dt)[:, None]
    ks = jnp.arange(TC + 1, dtype=F32)[:, None, None]
    mag = jnp.exp(ks * (a_re * dt)[None])
    ang = ks * (a_im * dt)[None]
    p_re, p_im = mag * jnp.cos(ang), mag * jnp.sin(ang)
    den = a_re * a_re + a_im * a_im
    x_re, x_im = p_re[1] - 1.0, p_im[1]
    f_re = (x_re * a_re + x_im * a_im) / den
    f_im = (x_im * a_re - x_re * a_im) / den
    bt_re, bt_im = jnp.swapaxes(b_re, 1, 2), jnp.swapaxes(b_im, 1, 2)
    bb_re = f_re[:, None, :] * bt_re - f_im[:, None, :] * bt_im
    bb_im = f_re[:, None, :] * bt_im + f_im[:, None, :] * bt_re
    cp_re = c_re[None] * p_re[:, :, None, :] - c_im[None] * p_im[:, :, None, :]
    cp_im = c_re[None] * p_im[:, :, None, :] + c_im[None] * p_re[:, :, None, :]
    kern = (jnp.einsum('gin,kgon->kgio', bb_re, cp_re[:TC], precision=hi)
            - jnp.einsum('gin,kgon->kgio', bb_im, cp_im[:TC], precision=hi))
    zero = jnp.zeros_like(kern[0])
    xt = jnp.stack([jnp.concatenate([kern[t - s] if t >= s else zero for t in range(TC)], axis=-1)
                    for s in range(TC)], axis=0)
    xt = xt.reshape(TC, nj, gpb * c, TC * c).swapaxes(0, 1).reshape(nj, TC * LANES, TC * c)
    q_re, q_im = p_re[TC - 1::-1][:, :, None, :], p_im[TC - 1::-1][:, :, None, :]
    xb = jnp.concatenate([q_re * bb_re[None] - q_im * bb_im[None], q_re * bb_im[None] + q_im * bb_re[None]], axis=-1)
    xb = xb.reshape(TC, nj, gpb * c, 2 * n_st).swapaxes(0, 1).reshape(nj, TC * LANES, 2 * n_st)
    xc = jnp.stack([cp_re[1:], -cp_im[1:]], axis=0)
    xc = jnp.transpose(xc, (2, 0, 4, 1, 3)).reshape(nj, gpb, 2, n_st, TC * c)
    xc = xc.swapaxes(1, 2).reshape(nj, 2 * gpb * n_st, TC * c)
    a_mat = jnp.concatenate([p_re[TC].reshape(nj, gpb * n_st), p_im[TC].reshape(nj, gpb * n_st)], axis=-1)
    a_mat = jnp.broadcast_to(a_mat[:, None, :], (nj, NB, a_mat.shape[-1]))
    n_state = 2 * gpb * n_st
    wtb, wc = pl.pallas_call(
        functools.partial(_expand_kernel, c=c, n_st=n_st),
        grid=(nj,),
        in_specs=[pl.BlockSpec((1,) + xt.shape[1:], lambda j: (j, 0, 0)),
                  pl.BlockSpec((1,) + xb.shape[1:], lambda j: (j, 0, 0)),
                  pl.BlockSpec((1,) + xc.shape[1:], lambda j: (j, 0, 0))],
        out_specs=[pl.BlockSpec((1, TC * LANES, TC * LANES + n_state), lambda j: (j, 0, 0)),
                   pl.BlockSpec((1, n_state, TC * LANES), lambda j: (j, 0, 0))],
        out_shape=[jax.ShapeDtypeStruct((nj, TC * LANES, TC * LANES + n_state), BF16),
                   jax.ShapeDtypeStruct((nj, n_state, TC * LANES), BF16)],
        compiler_params=pltpu.CompilerParams(dimension_semantics=("arbitrary",)),
        name="ssm_expand_call",
    )(xt, xb, xc)
    return wtb, wc, a_mat


def _pool_constants(w_pool, pool_w):
    n_grp, gw, _ = w_pool.shape
    wbd = jnp.zeros((pool_w, pool_w), F32)
    for gi in range(n_grp):
        wbd = wbd.at[gi * gw:(gi + 1) * gw, gi * gw:(gi + 1) * gw].set(w_pool[gi])
    win = jnp.repeat(jnp.asarray(POOL_WINDOWS, F32), gw)
    inv_w = (1.0 / win)[None, :]
    pos = jnp.repeat(jnp.arange(1, HALO + 1, dtype=F32), NB)[:, None]
    inv_c0 = 1.0 / jnp.minimum(pos, win[None, :])
    return wbd.astype(BF16), inv_w, inv_c0


def _layer(x, mem, g_pre, w_in, w_pool, pool_scale, a_re, a_im, log_dt, b_re, b_im, c_re, c_im, d_skip, w_glu,
           g_mem, w_kv, w_out, g_post):
    d = x.shape[-1]
    n_mem = mem.shape[1]
    pool_w = w_pool.shape[0] * w_pool.shape[1]
    ssm_w = d_skip.shape[0]
    att_w = w_kv.shape[1] // 2
    assert a_re.shape[0] * SSM_GROUP == ssm_w and b_re.shape[-1] == SSM_GROUP
    assert pool_w + ssm_w + att_w == d and w_pool.shape[0] == len(POOL_WINDOWS)

    kbd, vbd = _kv_call(mem, g_mem[None, :], w_kv[:, :att_w].T.astype(BF16), w_kv[:, att_w:].astype(BF16))
    wtb, wc, a_mat = _ssm_matrices(a_re, a_im, log_dt, b_re, b_im, c_re, c_im)
    wpool_bd, inv_w, inv_c0 = _pool_constants(w_pool, pool_w)
    return _layer_call(
        x, kbd, vbd, g_pre[None, :], w_in.astype(BF16), wpool_bd, pool_scale[None, :], inv_w, inv_c0,
        wtb, wc, a_mat, d_skip[None, :], w_glu.astype(BF16), w_out.astype(BF16), g_post[None, :],
        pool_w=pool_w, ssm_w=ssm_w, att_w=att_w, n_mem=n_mem)


def kernel(x, mem, g_pre, w_in, w_pool, pool_scale, a_re, a_im, log_dt, b_re, b_im, c_re, c_im, d_skip, w_glu, g_mem, w_kv, w_out, g_post):
    for i in range(g_pre.shape[0]):
        x = _layer(x, mem, g_pre[i], w_in[i], w_pool[i], pool_scale[i], a_re[i], a_im[i], log_dt[i], b_re[i], b_im[i],
                   c_re[i], c_im[i], d_skip[i], w_glu[i], g_mem[i], w_kv[i], w_out[i], g_post[i])
    return x
```

```python
import functools

import jax
import jax.numpy as jnp
from jax import lax
from jax.experimental import pallas as pl
from jax.experimental.pallas import tpu as pltpu

LANES = 128
NB = 8
TT = 128
TC = 4
BC = 2
TB = 16
KV_MB = 4
HALO = 16
POOL_WINDOWS = (2, 4, 8, 16)
SSM_GROUP = 16
MEM_HEADS = 4
EPS = 1e-6
VMEM_LIMIT_BYTES = 56 * 1024 * 1024

F32 = jnp.float32
BF16 = jnp.bfloat16


def _sigmoid(v):
    return 0.5 * jnp.tanh(0.5 * v) + 0.5


def _kv_kernel(mem_ref, gmem_ref, wkt_ref, wv_ref, kbd_ref, vbd_ref, *, att_w, n_mem):
    mb, _, d = mem_ref.shape
    m = mem_ref[...].reshape(mb * n_mem, d)
    ms = jnp.mean(m * m, axis=-1, keepdims=True)
    mn = ((m * lax.rsqrt(ms + EPS)) * gmem_ref[...]).astype(BF16)
    kt = lax.dot_general(wkt_ref[...], mn, (((1,), (1,)), ((), ())), preferred_element_type=F32)
    v = jnp.dot(mn, wv_ref[...], preferred_element_type=F32)
    hd = att_w // MEM_HEADS
    kt = kt * (hd ** -0.5)
    row_head = lax.broadcasted_iota(jnp.int32, (att_w, n_mem), 0) // hd
    col_head = lax.broadcasted_iota(jnp.int32, (n_mem, att_w), 1) // hd
    for i in range(mb):
        kt_i = kt[:, i * n_mem:(i + 1) * n_mem]
        v_i = v[i * n_mem:(i + 1) * n_mem, :]
        for hh in range(MEM_HEADS):
            kbd_ref[i, :, hh * n_mem:(hh + 1) * n_mem] = jnp.where(row_head == hh, kt_i, 0.0).astype(BF16)
            vbd_ref[i, hh * n_mem:(hh + 1) * n_mem, :] = jnp.where(col_head == hh, v_i, 0.0).astype(BF16)


def _kv_call(mem, g_mem, wkt, wv):
    bsz, n_mem, d = mem.shape
    att_w = wv.shape[1]
    assert bsz % KV_MB == 0
    kern = functools.partial(_kv_kernel, att_w=att_w, n_mem=n_mem)
    return pl.pallas_call(
        kern,
        grid=(bsz // KV_MB,),
        in_specs=[
            pl.BlockSpec((KV_MB, n_mem, d), lambda b: (b, 0, 0)),
            pl.BlockSpec((1, d), lambda b: (0, 0)),
            pl.BlockSpec((att_w, d), lambda b: (0, 0)),
            pl.BlockSpec((d, att_w), lambda b: (0, 0)),
        ],
        out_specs=[
            pl.BlockSpec((KV_MB, att_w, MEM_HEADS * n_mem), lambda b: (b, 0, 0)),
            pl.BlockSpec((KV_MB, MEM_HEADS * n_mem, att_w), lambda b: (b, 0, 0)),
        ],
        out_shape=[
            jax.ShapeDtypeStruct((bsz, att_w, MEM_HEADS * n_mem), BF16),
            jax.ShapeDtypeStruct((bsz, MEM_HEADS * n_mem, att_w), BF16),
        ],
        compiler_params=pltpu.CompilerParams(dimension_semantics=("arbitrary",)),
        name="kv_call",
    )(mem, g_mem, wkt, wv)


def _layer_kernel(x_ref, kbd_ref, vbd_ref, gpre_ref, win_ref, wpool_ref, pscale_ref, invw_ref, invc0_ref,
                  wtb_ref, wc_ref, a_ref, dskip_ref, wglu_ref, wout_ref, gpost_ref,
                  out_ref,
                  utb_ref, gate_ref, q_ref, ycat_ref, pd_ref, g_ref, hin_ref, hc_ref,
                  s_ref, win_s, wout_s, kbd_s,
                  *, d, pool_w, ssm_w, att_w, n_mem):
    bi = pl.program_id(0)
    ti = pl.program_id(1)
    n_pool = pool_w // LANES
    n_ssm = ssm_w // LANES
    n_slab = n_pool + n_ssm
    mix_w = pool_w + ssm_w
    n_chunk = TT // TC
    st = hc_ref.shape[-1] // 2
    hd = att_w // MEM_HEADS
    cur_rows = slice(HALO * NB, (HALO + TT) * NB)

    @pl.when((bi == 0) & (ti == 0))
    def _():
        pltpu.sync_copy(win_ref, win_s)
        pltpu.sync_copy(wout_ref, wout_s)

    @pl.when(ti == 0)
    def _():
        pltpu.sync_copy(kbd_ref.at[pl.ds(pl.multiple_of(bi * NB, NB), NB)], kbd_s)
        utb_ref[0:n_pool, 0:HALO * NB, :] = jnp.zeros((n_pool, HALO * NB, LANES), F32)
        hc_ref[...] = jnp.zeros(hc_ref.shape, F32)

    for c in range(NB // BC):
        crow = slice(c * BC * TT, (c + 1) * BC * TT)
        x = x_ref[c * BC:(c + 1) * BC].reshape(BC * TT, d)
        ms = jnp.mean(x * x, axis=-1, keepdims=True)
        h = ((x * lax.rsqrt(ms + EPS)) * gpre_ref[...]).astype(BF16)
        val = jnp.dot(h, win_s[:, 0:mix_w], preferred_element_type=F32)
        for bb in range(BC):
            for k in range(n_slab):
                utb_ref[k, pl.ds(HALO * NB + c * BC + bb, TT, stride=NB), :] = (
                    val[bb * TT:(bb + 1) * TT, k * LANES:(k + 1) * LANES])
        q_ref[crow, :] = jnp.dot(h, win_s[:, mix_w:mix_w + att_w], preferred_element_type=F32).astype(BF16)
        gate = jnp.dot(h, win_s[:, mix_w + att_w:], preferred_element_type=F32)
        gate_ref[crow, :] = (gate * _sigmoid(gate)).astype(BF16)

    y_intra = []
    for j in range(n_ssm):
        u = utb_ref[n_pool + j, cur_rows, :]
        u4 = u.reshape(n_chunk, TC, NB, LANES)
        up = jnp.concatenate([u4[:, s].reshape(n_chunk * NB, LANES) for s in range(TC)], axis=1).astype(BF16)
        yg = jnp.dot(up, wtb_ref[j], preferred_element_type=F32)
        y_intra.append(yg[:, 0:TC * LANES])
        g_ref[j] = yg[:, TC * LANES:]

    for b in range(NB):
        rows = slice(b * TT, (b + 1) * TT)
        s_ref[rows, :] = jnp.dot(q_ref[rows, :], kbd_s[b], preferred_element_type=F32)

    gw = pool_w // len(POOL_WINDOWS)
    lane = lax.broadcasted_iota(jnp.int32, (TB * NB, LANES), 1)
    for k in range(n_pool):
        w_lo = POOL_WINDOWS[(k * LANES) // gw]
        w_hi = POOL_WINDOWS[min(((k + 1) * LANES - 1) // gw, len(POOL_WINDOWS) - 1)]
        split = ((k * LANES) // gw + 1) * gw - k * LANES
        for i in range(TT // TB):
            base = (HALO + i * TB) * NB
            ext = utb_ref[k, base - (HALO - 1) * NB: base + TB * NB, :]
            sums = {1: ext}
            w = 1
            while w < w_hi:
                prev = sums[w]
                sums[2 * w] = prev[w * NB:] + prev[:-w * NB]
                w *= 2
            cur = ext[(HALO - 1) * NB:]
            lo = sums[w_lo][(HALO - w_lo) * NB:]
            hi = sums[w_hi][(HALO - w_hi) * NB:]
            ssum = jnp.where(lane < split, lo, hi)
            scale = invw_ref[:, k * LANES:(k + 1) * LANES]
            if i == 0:
                scale = jnp.where(ti == 0, invc0_ref[:, k * LANES:(k + 1) * LANES], scale)
            pd_ref[i * TB * NB:(i + 1) * TB * NB, k * LANES:(k + 1) * LANES] = (ssum * scale - cur).astype(BF16)
    for k in range(n_pool):
        utb_ref[k, 0:HALO * NB, :] = utb_ref[k, TT * NB:(TT + HALO) * NB, :]
    ypool = jnp.dot(pd_ref[...], wpool_ref[...], preferred_element_type=F32) * pscale_ref[...]
    for k in range(n_pool):
        utb_ref[k, cur_rows, :] = ypool[:, k * LANES:(k + 1) * LANES]

    a_re = [a_ref[j, :, 0:st] for j in range(n_ssm)]
    a_im = [a_ref[j, :, st:] for j in range(n_ssm)]
    h_re = [hc_ref[j, :, 0:st] for j in range(n_ssm)]
    h_im = [hc_ref[j, :, st:] for j in range(n_ssm)]
    for r in range(n_chunk):
        rows = slice(r * NB, (r + 1) * NB)
        for j in range(n_ssm):
            hin_ref[j, rows, 0:st] = h_re[j]
            hin_ref[j, rows, st:] = h_im[j]
            g_re = g_ref[j, rows, 0:st]
            g_im = g_ref[j, rows, st:]
            h_re[j], h_im[j] = (a_re[j] * h_re[j] - a_im[j] * h_im[j] + g_re,
                                a_re[j] * h_im[j] + a_im[j] * h_re[j] + g_im)
    for j in range(n_ssm):
        hc_ref[j, :, 0:st] = h_re[j]
        hc_ref[j, :, st:] = h_im[j]

    y_act = []
    for j in range(n_ssm):
        y_lin = y_intra[j] + jnp.dot(hin_ref[j].astype(BF16), wc_ref[j], preferred_element_type=F32)
        y_tb = jnp.stack([y_lin[:, s * LANES:(s + 1) * LANES].reshape(n_chunk, NB, LANES) for s in range(TC)],
                         axis=1).reshape(TT * NB, LANES)
        y = y_tb + dskip_ref[:, j * LANES:(j + 1) * LANES] * utb_ref[n_pool + j, cur_rows, :]
        y_act.append(jax.nn.gelu(y).astype(BF16))

    lane_t = lax.broadcasted_iota(jnp.int32, (TT, LANES), 1)
    heads_per_vreg = LANES // hd
    for b in range(NB):
        rows = slice(b * TT, (b + 1) * TT)
        es, inv = [], []
        for hh in range(MEM_HEADS):
            sh = s_ref[rows, hh * n_mem:(hh + 1) * n_mem]
            e = jnp.exp(sh - jnp.max(sh, axis=-1, keepdims=True))
            inv.append(1.0 / jnp.sum(e, axis=-1, keepdims=True))
            es.append(e.astype(BF16))
        o = jnp.dot(jnp.concatenate(es, axis=1), vbd_ref[b], preferred_element_type=F32)
        norm = []
        for c in range(att_w // LANES):
            sc = inv[c * heads_per_vreg]
            for e_i in range(1, heads_per_vreg):
                sc = jnp.where(lane_t < e_i * hd, sc, inv[c * heads_per_vreg + e_i])
            norm.append(sc)
        o = o * jnp.concatenate(norm, axis=1)
        ycat_ref[rows, mix_w:] = (o * gate_ref[rows, mix_w:].astype(F32)).astype(BF16)

    z = jnp.dot(jnp.concatenate(y_act, axis=1), wglu_ref[...], preferred_element_type=F32)
    y_ssm = z[:, 0:ssm_w] * _sigmoid(z[:, ssm_w:])
    for j in range(n_ssm):
        utb_ref[n_pool + j, cur_rows, :] = y_ssm[:, j * LANES:(j + 1) * LANES]

    for c in range(NB // BC):
        crow = slice(c * BC * TT, (c + 1) * BC * TT)
        for bb in range(BC):
            b = c * BC + bb
            rows = slice(b * TT, (b + 1) * TT)
            for k in range(n_slab):
                cols = slice(k * LANES, (k + 1) * LANES)
                yk = utb_ref[k, pl.ds(HALO * NB + b, TT, stride=NB), :]
                ycat_ref[rows, cols] = (yk * gate_ref[rows, cols].astype(F32)).astype(BF16)
        out = jnp.dot(ycat_ref[crow, :], wout_s[...], preferred_element_type=F32)
        ms2 = jnp.mean(out * out, axis=-1, keepdims=True)
        res = x_ref[c * BC:(c + 1) * BC].reshape(BC * TT, d) + (out * lax.rsqrt(ms2 + EPS)) * gpost_ref[...]
        out_ref[c * BC:(c + 1) * BC] = res.reshape(BC, TT, d)


def _const_spec(shape):
    zeros = (0,) * len(shape)
    return pl.BlockSpec(shape, lambda bi, ti: zeros, pipeline_mode=pl.Buffered(1))


def _layer_call(x, kbd, vbd, g_pre, w_in, wpool_bd, pscale, inv_w, inv_c0, wtb, wc, a_mat, d_skip, w_glu, w_out, g_post,
                *, pool_w, ssm_w, att_w, n_mem):
    bsz, seq, d = x.shape
    assert bsz % NB == 0 and seq % TT == 0 and TT % TB == 0 and TT % TC == 0 and TB == HALO and NB % BC == 0
    assert pool_w % LANES == 0 and ssm_w % LANES == 0 and att_w % LANES == 0
    n_pool, n_ssm = pool_w // LANES, ssm_w // LANES
    n_slab = n_pool + n_ssm
    n_state = a_mat.shape[-1]
    m_rows = NB * TT
    kern = functools.partial(_layer_kernel, d=d, pool_w=pool_w, ssm_w=ssm_w, att_w=att_w, n_mem=n_mem)
    consts = (g_pre, w_in, wpool_bd, pscale, inv_w, inv_c0, wtb, wc, a_mat, d_skip, w_glu, w_out, g_post)
    return pl.pallas_call(
        kern,
        grid=(bsz // NB, seq // TT),
        in_specs=[
            pl.BlockSpec((NB, TT, d), lambda bi, ti: (bi, ti, 0)),
            pl.BlockSpec(memory_space=pl.ANY),
            pl.BlockSpec((NB,) + vbd.shape[1:], lambda bi, ti: (bi, 0, 0), pipeline_mode=pl.Buffered(1)),
        ] + [pl.BlockSpec(memory_space=pl.ANY) if c is w_in or c is w_out else _const_spec(c.shape) for c in consts],
        out_specs=pl.BlockSpec((NB, TT, d), lambda bi, ti: (bi, ti, 0)),
        out_shape=jax.ShapeDtypeStruct(x.shape, x.dtype),
        scratch_shapes=[
            pltpu.VMEM((n_slab, (HALO + TT) * NB, LANES), F32),
            pltpu.VMEM((m_rows, d), BF16),
            pltpu.VMEM((m_rows, att_w), BF16),
            pltpu.VMEM((m_rows, d), BF16),
            pltpu.VMEM((m_rows, pool_w), BF16),
            pltpu.VMEM((n_ssm, m_rows // TC, n_state), F32),
            pltpu.VMEM((n_ssm, m_rows // TC, n_state), F32),
            pltpu.VMEM((n_ssm, NB, n_state), F32),
            pltpu.VMEM((m_rows, MEM_HEADS * n_mem), F32),
            pltpu.VMEM(w_in.shape, BF16),
            pltpu.VMEM(w_out.shape, BF16),
            pltpu.VMEM((NB,) + kbd.shape[1:], BF16),
        ],
        compiler_params=pltpu.CompilerParams(
            dimension_semantics=("arbitrary", "arbitrary"),
            vmem_limit_bytes=VMEM_LIMIT_BYTES,
        ),
        name="layer_call",
    )(x, kbd, vbd, *consts)


def _expand_kernel(xt_ref, xb_ref, xc_ref, wtb_ref, wc_ref, *, c, n_st):
    gpb = LANES // c
    tl = TC * LANES

    def rep(rows, cols, row_blk, col_blk, period):
        r = lax.broadcasted_iota(jnp.int32, (rows, cols), 0)
        q = lax.broadcasted_iota(jnp.int32, (rows, cols), 1)
        hit = (r // row_blk == q // col_blk) & (r % period == q % period)
        return jnp.where(hit, 1.0, 0.0).astype(BF16)

    def same_group(rows, cols, row_div, col_div):
        r = lax.broadcasted_iota(jnp.int32, (rows, cols), 0)
        q = lax.broadcasted_iota(jnp.int32, (rows, cols), 1)
        return (r // row_div) % gpb == (q // col_div) % gpb

    rep_t = rep(TC * c, tl, c, LANES, c)
    rep_b = rep(2 * n_st, 2 * gpb * n_st, n_st, gpb * n_st, n_st)
    wt = jnp.dot(xt_ref[0].astype(BF16), rep_t, preferred_element_type=F32)
    wtb_ref[0, :, 0:tl] = jnp.where(same_group(tl, tl, c, c), wt, 0.0).astype(BF16)
    wb = jnp.dot(xb_ref[0].astype(BF16), rep_b, preferred_element_type=F32)
    wtb_ref[0, :, tl:] = jnp.where(same_group(tl, 2 * gpb * n_st, c, n_st), wb, 0.0).astype(BF16)
    wcx = jnp.dot(xc_ref[0].astype(BF16), rep_t, preferred_element_type=F32)
    wc_ref[0] = jnp.where(same_group(2 * gpb * n_st, tl, n_st, c), wcx, 0.0).astype(BF16)


def _ssm_matrices(a_re, a_im, log_dt, b_re, b_im, c_re, c_im):
    n_grp, n_st = a_re.shape
    c = b_re.shape[-1]
    gpb = LANES // c
    nj = n_grp // gpb
    hi = lax.Precision.HIGHEST
    dt = jnp.exp(log_dt)[:, None]
    ks = jnp.arange(TC + 1, dtype=F32)[:, None, None]
    mag = jnp.exp(ks * (a_re * dt)[None])
    ang = ks * (a_im * dt)[None]
    p_re, p_im = mag * jnp.cos(ang), mag * jnp.sin(ang)
    den = a_re * a_re + a_im * a_im
    x_re, x_im = p_re[1] - 1.0, p_im[1]
    f_re = (x_re * a_re + x_im * a_im) / den
    f_im = (x_im * a_re - x_re * a_im) / den
    bt_re, bt_im = jnp.swapaxes(b_re, 1, 2), jnp.swapaxes(b_im, 1, 2)
    bb_re = f_re[:, None, :] * bt_re - f_im[:, None, :] * bt_im
    bb_im = f_re[:, None, :] * bt_im + f_im[:, None, :] * bt_re
    cp_re = c_re[None] * p_re[:, :, None, :] - c_im[None] * p_im[:, :, None, :]
    cp_im = c_re[None] * p_im[:, :, None, :] + c_im[None] * p_re[:, :, None, :]
    kern = (jnp.einsum('gin,kgon->kgio', bb_re, cp_re[:TC], precision=hi)
            - jnp.einsum('gin,kgon->kgio', bb_im, cp_im[:TC], precision=hi))
    zero = jnp.zeros_like(kern[0])
    xt = jnp.stack([jnp.concatenate([kern[t - s] if t >= s else zero for t in range(TC)], axis=-1)
                    for s in range(TC)], axis=0)
    xt = xt.reshape(TC, nj, gpb * c, TC * c).swapaxes(0, 1).reshape(nj, TC * LANES, TC * c)
    q_re, q_im = p_re[TC - 1::-1][:, :, None, :], p_im[TC - 1::-1][:, :, None, :]
    xb = jnp.concatenate([q_re * bb_re[None] - q_im * bb_im[None], q_re * bb_im[None] + q_im * bb_re[None]], axis=-1)
    xb = xb.reshape(TC, nj, gpb * c, 2 * n_st).swapaxes(0, 1).reshape(nj, TC * LANES, 2 * n_st)
    xc = jnp.stack([cp_re[1:], -cp_im[1:]], axis=0)
    xc = jnp.transpose(xc, (2, 0, 4, 1, 3)).reshape(nj, gpb, 2, n_st, TC * c)
    xc = xc.swapaxes(1, 2).reshape(nj, 2 * gpb * n_st, TC * c)
    a_mat = jnp.concatenate([p_re[TC].reshape(nj, gpb * n_st), p_im[TC].reshape(nj, gpb * n_st)], axis=-1)
    a_mat = jnp.broadcast_to(a_mat[:, None, :], (nj, NB, a_mat.shape[-1]))
    n_state = 2 * gpb * n_st
    wtb, wc = pl.pallas_call(
        functools.partial(_expand_kernel, c=c, n_st=n_st),
        grid=(nj,),
        in_specs=[pl.BlockSpec((1,) + xt.shape[1:], lambda j: (j, 0, 0)),
                  pl.BlockSpec((1,) + xb.shape[1:], lambda j: (j, 0, 0)),
                  pl.BlockSpec((1,) + xc.shape[1:], lambda j: (j, 0, 0))],
        out_specs=[pl.BlockSpec((1, TC * LANES, TC * LANES + n_state), lambda j: (j, 0, 0)),
                   pl.BlockSpec((1, n_state, TC * LANES), lambda j: (j, 0, 0))],
        out_shape=[jax.ShapeDtypeStruct((nj, TC * LANES, TC * LANES + n_state), BF16),
                   jax.ShapeDtypeStruct((nj, n_state, TC * LANES), BF16)],
        compiler_params=pltpu.CompilerParams(dimension_semantics=("arbitrary",)),
        name="ssm_expand_call",
    )(xt, xb, xc)
    return wtb, wc, a_mat


def _pool_constants(w_pool, pool_w):
    n_grp, gw, _ = w_pool.shape
    wbd = jnp.zeros((pool_w, pool_w), F32)
    for gi in range(n_grp):
        wbd = wbd.at[gi * gw:(gi + 1) * gw, gi * gw:(gi + 1) * gw].set(w_pool[gi])
    win = jnp.repeat(jnp.asarray(POOL_WINDOWS, F32), gw)
    inv_w = (1.0 / win)[None, :]
    pos = jnp.repeat(jnp.arange(1, HALO + 1, dtype=F32), NB)[:, None]
    inv_c0 = 1.0 / jnp.minimum(pos, win[None, :])
    return wbd.astype(BF16), inv_w, inv_c0


def _layer(x, mem, g_pre, w_in, w_pool, pool_scale, a_re, a_im, log_dt, b_re, b_im, c_re, c_im, d_skip, w_glu,
           g_mem, w_kv, w_out, g_post):
    d = x.shape[-1]
    n_mem = mem.shape[1]
    pool_w = w_pool.shape[0] * w_pool.shape[1]
    ssm_w = d_skip.shape[0]
    att_w = w_kv.shape[1] // 2
    assert a_re.shape[0] * SSM_GROUP == ssm_w and b_re.shape[-1] == SSM_GROUP
    assert pool_w + ssm_w + att_w == d and w_pool.shape[0] == len(POOL_WINDOWS)

    kbd, vbd = _kv_call(mem, g_mem[None, :], w_kv[:, :att_w].T.astype(BF16), w_kv[:, att_w:].astype(BF16))
    wtb, wc, a_mat = _ssm_matrices(a_re, a_im, log_dt, b_re, b_im, c_re, c_im)
    wpool_bd, inv_w, inv_c0 = _pool_constants(w_pool, pool_w)
    return _layer_call(
        x, kbd, vbd, g_pre[None, :], w_in.astype(BF16), wpool_bd, pool_scale[None, :], inv_w, inv_c0,
        wtb, wc, a_mat, d_skip[None, :], w_glu.astype(BF16), w_out.astype(BF16), g_post[None, :],
        pool_w=pool_w, ssm_w=ssm_w, att_w=att_w, n_mem=n_mem)


def kernel(x, mem, g_pre, w_in, w_pool, pool_scale, a_re, a_im, log_dt, b_re, b_im, c_re, c_im, d_skip, w_glu, g_mem, w_kv, w_out, g_post):
    for i in range(g_pre.shape[0]):
        x = _layer(x, mem, g_pre[i], w_in[i], w_pool[i], pool_scale[i], a_re[i], a_im[i], log_dt[i], b_re[i], b_im[i],
                   c_re[i], c_im[i], d_skip[i], w_glu[i], g_mem[i], w_kv[i], w_out[i], g_post[i])
    return x
```

```python
import functools

import jax
import jax.numpy as jnp
from jax import lax
from jax.experimental import pallas as pl
from jax.experimental.pallas import tpu as pltpu

LANES = 128
NB = 8
TT = 128
TC = 4
BC = 2
TB = 16
KV_MB = 4
HALO = 16
POOL_WINDOWS = (2, 4, 8, 16)
SSM_GROUP = 16
MEM_HEADS = 4
EPS = 1e-6
VMEM_LIMIT_BYTES = 60 * 1024 * 1024

F32 = jnp.float32
BF16 = jnp.bfloat16


def _sigmoid(v):
    return 0.5 * jnp.tanh(0.5 * v) + 0.5


def _kv_kernel(mem_ref, gmem_ref, wkt_ref, wv_ref, kbd_ref, vbd_ref, *, att_w, n_mem):
    mb, _, d = mem_ref.shape
    m = mem_ref[...].reshape(mb * n_mem, d)
    ms = jnp.mean(m * m, axis=-1, keepdims=True)
    mn = ((m * lax.rsqrt(ms + EPS)) * gmem_ref[...]).astype(BF16)
    kt = lax.dot_general(wkt_ref[...], mn, (((1,), (1,)), ((), ())), preferred_element_type=F32)
    v = jnp.dot(mn, wv_ref[...], preferred_element_type=F32)
    hd = att_w // MEM_HEADS
    kt = kt * (hd ** -0.5)
    row_head = lax.broadcasted_iota(jnp.int32, (att_w, n_mem), 0) // hd
    col_head = lax.broadcasted_iota(jnp.int32, (n_mem, att_w), 1) // hd
    for i in range(mb):
        kt_i = kt[:, i * n_mem:(i + 1) * n_mem]
        v_i = v[i * n_mem:(i + 1) * n_mem, :]
        for hh in range(MEM_HEADS):
            kbd_ref[i, :, hh * n_mem:(hh + 1) * n_mem] = jnp.where(row_head == hh, kt_i, 0.0).astype(BF16)
            vbd_ref[i, hh * n_mem:(hh + 1) * n_mem, :] = jnp.where(col_head == hh, v_i, 0.0).astype(BF16)


def _kv_call(mem, g_mem, wkt, wv):
    bsz, n_mem, d = mem.shape
    att_w = wv.shape[1]
    assert bsz % KV_MB == 0
    kern = functools.partial(_kv_kernel, att_w=att_w, n_mem=n_mem)
    return pl.pallas_call(
        kern,
        grid=(bsz // KV_MB,),
        in_specs=[
            pl.BlockSpec((KV_MB, n_mem, d), lambda b: (b, 0, 0)),
            pl.BlockSpec((1, d), lambda b: (0, 0)),
            pl.BlockSpec((att_w, d), lambda b: (0, 0)),
            pl.BlockSpec((d, att_w), lambda b: (0, 0)),
        ],
        out_specs=[
            pl.BlockSpec((KV_MB, att_w, MEM_HEADS * n_mem), lambda b: (b, 0, 0)),
            pl.BlockSpec((KV_MB, MEM_HEADS * n_mem, att_w), lambda b: (b, 0, 0)),
        ],
        out_shape=[
            jax.ShapeDtypeStruct((bsz, att_w, MEM_HEADS * n_mem), BF16),
            jax.ShapeDtypeStruct((bsz, MEM_HEADS * n_mem, att_w), BF16),
        ],
        compiler_params=pltpu.CompilerParams(dimension_semantics=("arbitrary",)),
        name="kv_call",
    )(mem, g_mem, wkt, wv)


def _layer_kernel(x_ref, kbd_ref, vbd_ref, gpre_ref, win_ref, pscale_ref, invw_ref, invc0_ref,
                  wtb_ref, wc_ref, a_ref, dskip_ref, wglu_ref, wout_ref, gpost_ref,
                  out_ref,
                  utb_ref, gate_ref, q_ref, ycat_ref, g_ref, hin_ref, hc_ref,
                  s_ref, win_s, wout_s, kbd_s,
                  *, d, pool_w, ssm_w, att_w, n_mem):
    bi = pl.program_id(0)
    ti = pl.program_id(1)
    n_pool = pool_w // LANES
    n_ssm = ssm_w // LANES
    n_slab = n_pool + n_ssm
    mix_w = pool_w + ssm_w
    n_chunk = TT // TC
    st = hc_ref.shape[-1] // 2
    hd = att_w // MEM_HEADS
    cur_rows = slice(HALO * NB, (HALO + TT) * NB)

    @pl.when((bi == 0) & (ti == 0))
    def _():
        pltpu.sync_copy(win_ref, win_s)
        pltpu.sync_copy(wout_ref, wout_s)

    @pl.when(ti == 0)
    def _():
        pltpu.sync_copy(kbd_ref.at[pl.ds(pl.multiple_of(bi * NB, NB), NB)], kbd_s)
        utb_ref[0:n_pool, 0:HALO * NB, :] = jnp.zeros((n_pool, HALO * NB, LANES), F32)
        hc_ref[...] = jnp.zeros(hc_ref.shape, F32)

    for c in range(NB // BC):
        crow = slice(c * BC * TT, (c + 1) * BC * TT)
        x = x_ref[c * BC:(c + 1) * BC].reshape(BC * TT, d)
        ms = jnp.mean(x * x, axis=-1, keepdims=True)
        h = ((x * lax.rsqrt(ms + EPS)) * gpre_ref[...]).astype(BF16)
        val = jnp.dot(h, win_s[:, 0:mix_w], preferred_element_type=F32)
        for bb in range(BC):
            for k in range(n_slab):
                utb_ref[k, pl.ds(HALO * NB + c * BC + bb, TT, stride=NB), :] = (
                    val[bb * TT:(bb + 1) * TT, k * LANES:(k + 1) * LANES])
        q_ref[crow, :] = jnp.dot(h, win_s[:, mix_w:mix_w + att_w], preferred_element_type=F32).astype(BF16)
        gate = jnp.dot(h, win_s[:, mix_w + att_w:], preferred_element_type=F32)
        gate_ref[crow, :] = (gate * _sigmoid(gate)).astype(BF16)

    y_intra = []
    for j in range(n_ssm):
        u = utb_ref[n_pool + j, cur_rows, :]
        u4 = u.reshape(n_chunk, TC, NB, LANES)
        up = jnp.concatenate([u4[:, s].reshape(n_chunk * NB, LANES) for s in range(TC)], axis=1).astype(BF16)
        yg = jnp.dot(up, wtb_ref[j], preferred_element_type=F32)
        y_intra.append(yg[:, 0:TC * LANES])
        g_ref[j] = yg[:, TC * LANES:]

    for b in range(NB):
        rows = slice(b * TT, (b + 1) * TT)
        s_ref[rows, :] = jnp.dot(q_ref[rows, :], kbd_s[b], preferred_element_type=F32)

    gw = pool_w // len(POOL_WINDOWS)
    lane = lax.broadcasted_iota(jnp.int32, (TB * NB, LANES), 1)
    for k in range(n_pool):
        w_lo = POOL_WINDOWS[(k * LANES) // gw]
        w_hi = POOL_WINDOWS[min(((k + 1) * LANES - 1) // gw, len(POOL_WINDOWS) - 1)]
        split = ((k * LANES) // gw + 1) * gw - k * LANES
        nxt = utb_ref[k, TT * NB:(TT + HALO) * NB, :]
        for i in reversed(range(TT // TB)):
            base = (HALO + i * TB) * NB
            ext = utb_ref[k, base - (HALO - 1) * NB: base + TB * NB, :]
            sums = {1: ext}
            w = 1
            while w < w_hi:
                prev = sums[w]
                sums[2 * w] = prev[w * NB:] + prev[:-w * NB]
                w *= 2
            cur = ext[(HALO - 1) * NB:]
            lo = sums[w_lo][(HALO - w_lo) * NB:]
            hi = sums[w_hi][(HALO - w_hi) * NB:]
            ssum = jnp.where(lane < split, lo, hi)
            scale = invw_ref[:, k * LANES:(k + 1) * LANES]
            if i == 0:
                scale = jnp.where(ti == 0, invc0_ref[:, k * LANES:(k + 1) * LANES], scale)
            utb_ref[k, base:base + TB * NB, :] = (ssum * scale - cur) * pscale_ref[:, k * LANES:(k + 1) * LANES]
        utb_ref[k, 0:HALO * NB, :] = nxt
    for b in range(NB):
        rows = slice(b * TT, (b + 1) * TT)
        for k in range(n_pool):
            cols = slice(k * LANES, (k + 1) * LANES)
            yk = utb_ref[k, pl.ds(HALO * NB + b, TT, stride=NB), :]
            ycat_ref[rows, cols] = (yk * gate_ref[rows, cols].astype(F32)).astype(BF16)

    a_re = [a_ref[j, :, 0:st] for j in range(n_ssm)]
    a_im = [a_ref[j, :, st:] for j in range(n_ssm)]
    h_re = [hc_ref[j, :, 0:st] for j in range(n_ssm)]
    h_im = [hc_ref[j, :, st:] for j in range(n_ssm)]
    for r in range(n_chunk):
        rows = slice(r * NB, (r + 1) * NB)
        for j in range(n_ssm):
            hin_ref[j, rows, 0:st] = h_re[j]
            hin_ref[j, rows, st:] = h_im[j]
            g_re = g_ref[j, rows, 0:st]
            g_im = g_ref[j, rows, st:]
            h_re[j], h_im[j] = (a_re[j] * h_re[j] - a_im[j] * h_im[j] + g_re,
                                a_re[j] * h_im[j] + a_im[j] * h_re[j] + g_im)
    for j in range(n_ssm):
        hc_ref[j, :, 0:st] = h_re[j]
        hc_ref[j, :, st:] = h_im[j]

    y_act = []
    for j in range(n_ssm):
        y_lin = y_intra[j] + jnp.dot(hin_ref[j].astype(BF16), wc_ref[j], preferred_element_type=F32)
        y_tb = jnp.stack([y_lin[:, s * LANES:(s + 1) * LANES].reshape(n_chunk, NB, LANES) for s in range(TC)],
                         axis=1).reshape(TT * NB, LANES)
        y = y_tb + dskip_ref[:, j * LANES:(j + 1) * LANES] * utb_ref[n_pool + j, cur_rows, :]
        y_act.append(jax.nn.gelu(y).astype(BF16))

    lane_t = lax.broadcasted_iota(jnp.int32, (TT, LANES), 1)
    heads_per_vreg = LANES // hd
    def attend(b):
        rows = slice(b * TT, (b + 1) * TT)
        es, inv = [], []
        for hh in range(MEM_HEADS):
            sh = s_ref[rows, hh * n_mem:(hh + 1) * n_mem]
            e = jnp.exp(sh - jnp.max(sh, axis=-1, keepdims=True))
            inv.append(1.0 / jnp.sum(e, axis=-1, keepdims=True))
            es.append(e.astype(BF16))
        o = jnp.dot(jnp.concatenate(es, axis=1), vbd_ref[b], preferred_element_type=F32)
        norm = []
        for c in range(att_w // LANES):
            sc = inv[c * heads_per_vreg]
            for e_i in range(1, heads_per_vreg):
                sc = jnp.where(lane_t < e_i * hd, sc, inv[c * heads_per_vreg + e_i])
            norm.append(sc)
        o = o * jnp.concatenate(norm, axis=1)
        ycat_ref[rows, mix_w:] = (o * gate_ref[rows, mix_w:].astype(F32)).astype(BF16)

    for b in range(NB // 2):
        attend(b)

    z = jnp.dot(jnp.concatenate(y_act, axis=1), wglu_ref[...], preferred_element_type=F32)
    y_ssm = z[:, 0:ssm_w] * _sigmoid(z[:, ssm_w:])
    for j in range(n_ssm):
        utb_ref[n_pool + j, cur_rows, :] = y_ssm[:, j * LANES:(j + 1) * LANES]

    for b in range(NB // 2, NB):
        attend(b)

    for b in range(NB):
        rows = slice(b * TT, (b + 1) * TT)
        for k in range(n_pool, n_slab):
            cols = slice(k * LANES, (k + 1) * LANES)
            yk = utb_ref[k, pl.ds(HALO * NB + b, TT, stride=NB), :]
            ycat_ref[rows, cols] = (yk * gate_ref[rows, cols].astype(F32)).astype(BF16)

    def project(c):
        crow = slice(c * BC * TT, (c + 1) * BC * TT)
        s_ref[crow, :] = jnp.dot(ycat_ref[crow, :], wout_s[...], preferred_element_type=F32)

    def finish(c):
        out = s_ref[pl.ds(pl.multiple_of(c * BC * TT + jnp.minimum(ti, 0) * 8, 8), BC * TT), :]
        ms2 = jnp.mean(out * out, axis=-1, keepdims=True)
        res = x_ref[c * BC:(c + 1) * BC].reshape(BC * TT, d) + (out * lax.rsqrt(ms2 + EPS)) * gpost_ref[...]
        out_ref[c * BC:(c + 1) * BC] = res.reshape(BC, TT, d)

    n_c = NB // BC
    project(0)
    for c in range(n_c):
        if c + 1 < n_c:
            project(c + 1)
        finish(c)


def _const_spec(shape):
    zeros = (0,) * len(shape)
    return pl.BlockSpec(shape, lambda bi, ti: zeros, pipeline_mode=pl.Buffered(1))


def _layer_call(x, kbd, vbd, g_pre, w_in, pscale, inv_w, inv_c0, wtb, wc, a_mat, d_skip, w_glu, w_out, g_post,
                *, pool_w, ssm_w, att_w, n_mem):
    bsz, seq, d = x.shape
    assert bsz % NB == 0 and seq % TT == 0 and TT % TB == 0 and TT % TC == 0 and TB == HALO and NB % BC == 0
    assert pool_w % LANES == 0 and ssm_w % LANES == 0 and att_w % LANES == 0
    n_pool, n_ssm = pool_w // LANES, ssm_w // LANES
    n_slab = n_pool + n_ssm
    n_state = a_mat.shape[-1]
    m_rows = NB * TT
    kern = functools.partial(_layer_kernel, d=d, pool_w=pool_w, ssm_w=ssm_w, att_w=att_w, n_mem=n_mem)
    consts = (g_pre, w_in, pscale, inv_w, inv_c0, wtb, wc, a_mat, d_skip, w_glu, w_out, g_post)
    return pl.pallas_call(
        kern,
        grid=(bsz // NB, seq // TT),
        in_specs=[
            pl.BlockSpec((NB, TT, d), lambda bi, ti: (bi, ti, 0)),
            pl.BlockSpec(memory_space=pl.ANY),
            pl.BlockSpec((NB,) + vbd.shape[1:], lambda bi, ti: (bi, 0, 0), pipeline_mode=pl.Buffered(1)),
        ] + [pl.BlockSpec(memory_space=pl.ANY) if c is w_in or c is w_out else _const_spec(c.shape) for c in consts],
        out_specs=pl.BlockSpec((NB, TT, d), lambda bi, ti: (bi, ti, 0)),
        out_shape=jax.ShapeDtypeStruct(x.shape, x.dtype),
        scratch_shapes=[
            pltpu.VMEM((n_slab, (HALO + TT) * NB, LANES), F32),
            pltpu.VMEM((m_rows, d), BF16),
            pltpu.VMEM((m_rows, att_w), BF16),
            pltpu.VMEM((m_rows, d), BF16),
            pltpu.VMEM((n_ssm, m_rows // TC, n_state), F32),
            pltpu.VMEM((n_ssm, m_rows // TC, n_state), F32),
            pltpu.VMEM((n_ssm, NB, n_state), F32),
            pltpu.VMEM((m_rows, MEM_HEADS * n_mem), F32),
            pltpu.VMEM(w_in.shape, BF16),
            pltpu.VMEM(w_out.shape, BF16),
            pltpu.VMEM((NB,) + kbd.shape[1:], BF16),
        ],
        compiler_params=pltpu.CompilerParams(
            dimension_semantics=("arbitrary", "arbitrary"),
            vmem_limit_bytes=VMEM_LIMIT_BYTES,
        ),
        name="layer_call",
    )(x, kbd, vbd, *consts)


def _expand_kernel(xt_ref, xb_ref, xc_ref, wtb_ref, wc_ref, *, c, n_st):
    gpb = LANES // c
    tl = TC * LANES

    def rep(rows, cols, row_blk, col_blk, period):
        r = lax.broadcasted_iota(jnp.int32, (rows, cols), 0)
        q = lax.broadcasted_iota(jnp.int32, (rows, cols), 1)
        hit = (r // row_blk == q // col_blk) & (r % period == q % period)
        return jnp.where(hit, 1.0, 0.0).astype(BF16)

    def same_group(rows, cols, row_div, col_div):
        r = lax.broadcasted_iota(jnp.int32, (rows, cols), 0)
        q = lax.broadcasted_iota(jnp.int32, (rows, cols), 1)
        return (r // row_div) % gpb == (q // col_div) % gpb

    rep_t = rep(TC * c, tl, c, LANES, c)
    rep_b = rep(2 * n_st, 2 * gpb * n_st, n_st, gpb * n_st, n_st)
    wt = jnp.dot(xt_ref[0].astype(BF16), rep_t, preferred_element_type=F32)
    wtb_ref[0, :, 0:tl] = jnp.where(same_group(tl, tl, c, c), wt, 0.0).astype(BF16)
    wb = jnp.dot(xb_ref[0].astype(BF16), rep_b, preferred_element_type=F32)
    wtb_ref[0, :, tl:] = jnp.where(same_group(tl, 2 * gpb * n_st, c, n_st), wb, 0.0).astype(BF16)
    wcx = jnp.dot(xc_ref[0].astype(BF16), rep_t, preferred_element_type=F32)
    wc_ref[0] = jnp.where(same_group(2 * gpb * n_st, tl, n_st, c), wcx, 0.0).astype(BF16)


def _ssm_matrices(a_re, a_im, log_dt, b_re, b_im, c_re, c_im):
    n_grp, n_st = a_re.shape
    c = b_re.shape[-1]
    gpb = LANES // c
    nj = n_grp // gpb
    hi = lax.Precision.HIGHEST
    dt = jnp.exp(log_dt)[:, None]
    ks = jnp.arange(TC + 1, dtype=F32)[:, None, None]
    mag = jnp.exp(ks * (a_re * dt)[None])
    ang = ks * (a_im * dt)[None]
    p_re, p_im = mag * jnp.cos(ang), mag * jnp.sin(ang)
    den = a_re * a_re + a_im * a_im
    x_re, x_im = p_re[1] - 1.0, p_im[1]
    f_re = (x_re * a_re + x_im * a_im) / den
    f_im = (x_im * a_re - x_re * a_im) / den
    bt_re, bt_im = jnp.swapaxes(b_re, 1, 2), jnp.swapaxes(b_im, 1, 2)
    bb_re = f_re[:, None, :] * bt_re - f_im[:, None, :] * bt_im
    bb_im = f_re[:, None, :] * bt_im + f_im[:, None, :] * bt_re
    cp_re = c_re[None] * p_re[:, :, None, :] - c_im[None] * p_im[:, :, None, :]
    cp_im = c_re[None] * p_im[:, :, None, :] + c_im[None] * p_re[:, :, None, :]
    kern = (jnp.einsum('gin,kgon->kgio', bb_re, cp_re[:TC], precision=hi)
            - jnp.einsum('gin,kgon->kgio', bb_im, cp_im[:TC], precision=hi))
    zero = jnp.zeros_like(kern[0])
    xt = jnp.stack([jnp.concatenate([kern[t - s] if t >= s else zero for t in range(TC)], axis=-1)
                    for s in range(TC)], axis=0)
    xt = xt.reshape(TC, nj, gpb * c, TC * c).swapaxes(0, 1).reshape(nj, TC * LANES, TC * c)
    q_re, q_im = p_re[TC - 1::-1][:, :, None, :], p_im[TC - 1::-1][:, :, None, :]
    xb = jnp.concatenate([q_re * bb_re[None] - q_im * bb_im[None], q_re * bb_im[None] + q_im * bb_re[None]], axis=-1)
    xb = xb.reshape(TC, nj, gpb * c, 2 * n_st).swapaxes(0, 1).reshape(nj, TC * LANES, 2 * n_st)
    xc = jnp.stack([cp_re[1:], -cp_im[1:]], axis=0)
    xc = jnp.transpose(xc, (2, 0, 4, 1, 3)).reshape(nj, gpb, 2, n_st, TC * c)
    xc = xc.swapaxes(1, 2).reshape(nj, 2 * gpb * n_st, TC * c)
    a_mat = jnp.concatenate([p_re[TC].reshape(nj, gpb * n_st), p_im[TC].reshape(nj, gpb * n_st)], axis=-1)
    a_mat = jnp.broadcast_to(a_mat[:, None, :], (nj, NB, a_mat.shape[-1]))
    n_state = 2 * gpb * n_st
    wtb, wc = pl.pallas_call(
        functools.partial(_expand_kernel, c=c, n_st=n_st),
        grid=(nj,),
        in_specs=[pl.BlockSpec((1,) + xt.shape[1:], lambda j: (j, 0, 0)),
                  pl.BlockSpec((1,) + xb.shape[1:], lambda j: (j, 0, 0)),
                  pl.BlockSpec((1,) + xc.shape[1:], lambda j: (j, 0, 0))],
        out_specs=[pl.BlockSpec((1, TC * LANES, TC * LANES + n_state), lambda j: (j, 0, 0)),
                   pl.BlockSpec((1, n_state, TC * LANES), lambda j: (j, 0, 0))],
        out_shape=[jax.ShapeDtypeStruct((nj, TC * LANES, TC * LANES + n_state), BF16),
                   jax.ShapeDtypeStruct((nj, n_state, TC * LANES), BF16)],
        compiler_params=pltpu.CompilerParams(dimension_semantics=("arbitrary",)),
        name="ssm_expand_call",
    )(xt, xb, xc)
    return wtb, wc, a_mat


def _pool_constants(w_pool, pool_w):
    n_grp, gw, _ = w_pool.shape
    wbd = jnp.zeros((pool_w, pool_w), F32)
    for gi in range(n_grp):
        wbd = wbd.at[gi * gw:(gi + 1) * gw, gi * gw:(gi + 1) * gw].set(w_pool[gi])
    win = jnp.repeat(jnp.asarray(POOL_WINDOWS, F32), gw)
    inv_w = (1.0 / win)[None, :]
    pos = jnp.repeat(jnp.arange(1, HALO + 1, dtype=F32), NB)[:, None]
    inv_c0 = 1.0 / jnp.minimum(pos, win[None, :])
    return wbd, inv_w, inv_c0


def _layer(x, mem, g_pre, w_in, w_pool, pool_scale, a_re, a_im, log_dt, b_re, b_im, c_re, c_im, d_skip, w_glu,
           g_mem, w_kv, w_out, g_post):
    d = x.shape[-1]
    n_mem = mem.shape[1]
    pool_w = w_pool.shape[0] * w_pool.shape[1]
    ssm_w = d_skip.shape[0]
    att_w = w_kv.shape[1] // 2
    assert a_re.shape[0] * SSM_GROUP == ssm_w and b_re.shape[-1] == SSM_GROUP
    assert pool_w + ssm_w + att_w == d and w_pool.shape[0] == len(POOL_WINDOWS)

    kbd, vbd = _kv_call(mem, g_mem[None, :], w_kv[:, :att_w].T.astype(BF16), w_kv[:, att_w:].astype(BF16))
    wtb, wc, a_mat = _ssm_matrices(a_re, a_im, log_dt, b_re, b_im, c_re, c_im)
    wpool_bd, inv_w, inv_c0 = _pool_constants(w_pool, pool_w)
    w_pool_in = jnp.dot(w_in[:, :pool_w], wpool_bd, precision=lax.Precision.HIGHEST)
    w_in = jnp.concatenate([w_pool_in, w_in[:, pool_w:]], axis=1)
    return _layer_call(
        x, kbd, vbd, g_pre[None, :], w_in.astype(BF16), pool_scale[None, :], inv_w, inv_c0,
        wtb, wc, a_mat, d_skip[None, :], w_glu.astype(BF16), w_out.astype(BF16), g_post[None, :],
        pool_w=pool_w, ssm_w=ssm_w, att_w=att_w, n_mem=n_mem)


def kernel(x, mem, g_pre, w_in, w_pool, pool_scale, a_re, a_im, log_dt, b_re, b_im, c_re, c_im, d_skip, w_glu, g_mem, w_kv, w_out, g_post):
    for i in range(g_pre.shape[0]):
        x = _layer(x, mem, g_pre[i], w_in[i], w_pool[i], pool_scale[i], a_re[i], a_im[i], log_dt[i], b_re[i], b_im[i],
                   c_re[i], c_im[i], d_skip[i], w_glu[i], g_mem[i], w_kv[i], w_out[i], g_post[i])
    return x
```

```python
import functools

import jax
import jax.numpy as jnp
from jax import lax
from jax.experimental import pallas as pl
from jax.experimental.pallas import tpu as pltpu

LANES = 128
NB = 8
TT = 128
TC = 4
BC = 2
TB = 16
KV_MB = 4
HALO = 16
POOL_WINDOWS = (2, 4, 8, 16)
SSM_GROUP = 16
MEM_HEADS = 4
EPS = 1e-6
VMEM_LIMIT_BYTES = 60 * 1024 * 1024

F32 = jnp.float32
BF16 = jnp.bfloat16


def _sigmoid(v):
    return 0.5 * jnp.tanh(0.5 * v) + 0.5


def _kv_kernel(mem_ref, gmem_ref, wkt_ref, wv_ref, kt_ref, v_ref, *, att_w, n_mem):
    mb, _, d = mem_ref.shape
    m = mem_ref[...].reshape(mb * n_mem, d)
    ms = jnp.mean(m * m, axis=-1, keepdims=True)
    mn = ((m * lax.rsqrt(ms + EPS)) * gmem_ref[...]).astype(BF16)
    kt = lax.dot_general(wkt_ref[...], mn, (((1,), (1,)), ((), ())), preferred_element_type=F32)
    v = jnp.dot(mn, wv_ref[...], preferred_element_type=F32)
    hd = att_w // MEM_HEADS
    kt = kt * (hd ** -0.5)
    for i in range(mb):
        kt_ref[i] = kt[:, i * n_mem:(i + 1) * n_mem].astype(BF16)
        v_ref[i] = v[i * n_mem:(i + 1) * n_mem, :].astype(BF16)


def _kv_call(mem, g_mem, wkt, wv):
    bsz, n_mem, d = mem.shape
    att_w = wv.shape[1]
    assert bsz % KV_MB == 0
    kern = functools.partial(_kv_kernel, att_w=att_w, n_mem=n_mem)
    return pl.pallas_call(
        kern,
        grid=(bsz // KV_MB,),
        in_specs=[
            pl.BlockSpec((KV_MB, n_mem, d), lambda b: (b, 0, 0)),
            pl.BlockSpec((1, d), lambda b: (0, 0)),
            pl.BlockSpec((att_w, d), lambda b: (0, 0)),
            pl.BlockSpec((d, att_w), lambda b: (0, 0)),
        ],
        out_specs=[
            pl.BlockSpec((KV_MB, att_w, n_mem), lambda b: (b, 0, 0)),
            pl.BlockSpec((KV_MB, n_mem, att_w), lambda b: (b, 0, 0)),
        ],
        out_shape=[
            jax.ShapeDtypeStruct((bsz, att_w, n_mem), BF16),
            jax.ShapeDtypeStruct((bsz, n_mem, att_w), BF16),
        ],
        compiler_params=pltpu.CompilerParams(dimension_semantics=("arbitrary",)),
        name="kv_call",
    )(mem, g_mem, wkt, wv)


def _layer_kernel(xn_ref, x_any, kt_ref, v_ref, gpre_ref, win_ref, wpool_ref, pscale_ref, invw_ref, invc0_ref,
                  wtb_ref, wc_ref, a_ref, dskip_ref, wglu_ref, wout_ref, gpost_ref,
                  out_ref,
                  utb_ref, gate_ref, q_ref, ycat_ref, g_ref, hin_ref, hc_ref,
                  s_ref, win_s, wout_s, kt_s, v_s, hmask_s, xres_ref,
                  *, d, pool_w, ssm_w, att_w, n_mem):
    bi = pl.program_id(0)
    ti = pl.program_id(1)
    n_pool = pool_w // LANES
    n_ssm = ssm_w // LANES
    n_slab = n_pool + n_ssm
    mix_w = pool_w + ssm_w
    n_chunk = TT // TC
    st = hc_ref.shape[-1] // 2
    hd = att_w // MEM_HEADS
    cur_rows = slice(HALO * NB, (HALO + TT) * NB)

    def front(c, src_ref):
        crow = slice(c * BC * TT, (c + 1) * BC * TT)
        xc = src_ref[c * BC:(c + 1) * BC]
        x = xc.reshape(BC * TT, d)
        ms = jnp.mean(x * x, axis=-1, keepdims=True)
        h = ((x * lax.rsqrt(ms + EPS)) * gpre_ref[...]).astype(BF16)
        val = jnp.dot(h, win_s[:, 0:mix_w], preferred_element_type=F32)
        for bb in range(BC):
            for k in range(n_slab):
                utb_ref[k, pl.ds(HALO * NB + c * BC + bb, TT, stride=NB), :] = (
                    val[bb * TT:(bb + 1) * TT, k * LANES:(k + 1) * LANES])
        q_ref[crow, :] = jnp.dot(h, win_s[:, mix_w:mix_w + att_w], preferred_element_type=F32).astype(BF16)
        gate = jnp.dot(h, win_s[:, mix_w + att_w:], preferred_element_type=F32)
        gate_ref[crow, :] = (gate * _sigmoid(gate)).astype(BF16)
        if src_ref is not xres_ref:
            xres_ref[c * BC:(c + 1) * BC] = xc

    @pl.when((bi == 0) & (ti == 0))
    def _():
        pltpu.sync_copy(win_ref, win_s)
        pltpu.sync_copy(wout_ref, wout_s)
        pltpu.sync_copy(x_any.at[pl.ds(0, NB), pl.ds(0, TT), :], xres_ref)
        for c in range(NB // BC):
            front(c, xres_ref)
        row_head = lax.broadcasted_iota(jnp.int32, (att_w, n_mem), 0) // hd
        col_head = lax.broadcasted_iota(jnp.int32, (n_mem, att_w), 1) // hd
        for hh in range(MEM_HEADS):
            hmask_s[0, hh] = jnp.where(row_head == hh, 1.0, 0.0).astype(BF16)
            hmask_s[1, hh] = jnp.where(col_head == hh, 1.0, 0.0).astype(BF16)

    @pl.when(ti == 0)
    def _():
        pltpu.sync_copy(kt_ref.at[pl.ds(pl.multiple_of(bi * NB, NB), NB)], kt_s)
        pltpu.sync_copy(v_ref.at[pl.ds(pl.multiple_of(bi * NB, NB), NB)], v_s)
        utb_ref[0:n_pool, 0:HALO * NB, :] = jnp.zeros((n_pool, HALO * NB, LANES), F32)
        hc_ref[...] = jnp.zeros(hc_ref.shape, F32)

    y_intra = []
    for j in range(n_ssm):
        u = utb_ref[n_pool + j, cur_rows, :]
        u4 = u.reshape(n_chunk, TC, NB, LANES)
        up = jnp.concatenate([u4[:, s].reshape(n_chunk * NB, LANES) for s in range(TC)], axis=1).astype(BF16)
        yg = jnp.dot(up, wtb_ref[j], preferred_element_type=F32)
        y_intra.append(yg[:, 0:TC * LANES])
        g_ref[j] = yg[:, TC * LANES:]

    for b in range(NB):
        rows = slice(b * TT, (b + 1) * TT)
        kbd = jnp.concatenate([kt_s[b] * hmask_s[0, hh] for hh in range(MEM_HEADS)], axis=1)
        s_ref[rows, :] = jnp.dot(q_ref[rows, :], kbd, preferred_element_type=F32)

    gw = pool_w // len(POOL_WINDOWS)
    lane = lax.broadcasted_iota(jnp.int32, (TB * NB, LANES), 1)
    for k in range(n_pool):
        w_lo = POOL_WINDOWS[(k * LANES) // gw]
        w_hi = POOL_WINDOWS[min(((k + 1) * LANES - 1) // gw, len(POOL_WINDOWS) - 1)]
        split = ((k * LANES) // gw + 1) * gw - k * LANES
        for i in range(TT // TB):
            base = (HALO + i * TB) * NB
            ext = utb_ref[k, base - (HALO - 1) * NB: base + TB * NB, :]
            sums = {1: ext}
            w = 1
            while w < w_hi:
                prev = sums[w]
                sums[2 * w] = prev[w * NB:] + prev[:-w * NB]
                w *= 2
            cur = ext[(HALO - 1) * NB:]
            lo = sums[w_lo][(HALO - w_lo) * NB:]
            hi = sums[w_hi][(HALO - w_hi) * NB:]
            ssum = jnp.where(lane < split, lo, hi)
            scale = invw_ref[:, k * LANES:(k + 1) * LANES]
            if i == 0:
                scale = jnp.where(ti == 0, invc0_ref[:, k * LANES:(k + 1) * LANES], scale)
            ycat_ref[i * TB * NB:(i + 1) * TB * NB, k * LANES:(k + 1) * LANES] = (ssum * scale - cur).astype(BF16)
    for k in range(n_pool):
        utb_ref[k, 0:HALO * NB, :] = utb_ref[k, TT * NB:(TT + HALO) * NB, :]
    ypool = jnp.dot(ycat_ref[:, 0:pool_w], wpool_ref[...], preferred_element_type=F32) * pscale_ref[...]
    for k in range(n_pool):
        utb_ref[k, cur_rows, :] = ypool[:, k * LANES:(k + 1) * LANES]

    a_re = [a_ref[j, :, 0:st] for j in range(n_ssm)]
    a_im = [a_ref[j, :, st:] for j in range(n_ssm)]
    h_re = [hc_ref[j, :, 0:st] for j in range(n_ssm)]
    h_im = [hc_ref[j, :, st:] for j in range(n_ssm)]
    for r in range(n_chunk):
        rows = slice(r * NB, (r + 1) * NB)
        for j in range(n_ssm):
            hin_ref[j, rows, 0:st] = h_re[j]
            hin_ref[j, rows, st:] = h_im[j]
            g_re = g_ref[j, rows, 0:st]
            g_im = g_ref[j, rows, st:]
            h_re[j], h_im[j] = (a_re[j] * h_re[j] - a_im[j] * h_im[j] + g_re,
                                a_re[j] * h_im[j] + a_im[j] * h_re[j] + g_im)
    for j in range(n_ssm):
        hc_ref[j, :, 0:st] = h_re[j]
        hc_ref[j, :, st:] = h_im[j]

    y_act = []
    for j in range(n_ssm):
        y_lin = y_intra[j] + jnp.dot(hin_ref[j].astype(BF16), wc_ref[j], preferred_element_type=F32)
        y_tb = jnp.stack([y_lin[:, s * LANES:(s + 1) * LANES].reshape(n_chunk, NB, LANES) for s in range(TC)],
                         axis=1).reshape(TT * NB, LANES)
        y = y_tb + dskip_ref[:, j * LANES:(j + 1) * LANES] * utb_ref[n_pool + j, cur_rows, :]
        y_act.append(jax.nn.gelu(y).astype(BF16))

    lane_t = lax.broadcasted_iota(jnp.int32, (TT, LANES), 1)
    heads_per_vreg = LANES // hd
    for b in range(NB):
        rows = slice(b * TT, (b + 1) * TT)
        es, inv = [], []
        for hh in range(MEM_HEADS):
            sh = s_ref[rows, hh * n_mem:(hh + 1) * n_mem]
            e = jnp.exp(sh - jnp.max(sh, axis=-1, keepdims=True))
            inv.append(1.0 / jnp.sum(e, axis=-1, keepdims=True))
            es.append(e.astype(BF16))
        vbd = jnp.concatenate([v_s[b] * hmask_s[1, hh] for hh in range(MEM_HEADS)], axis=0)
        o = jnp.dot(jnp.concatenate(es, axis=1), vbd, preferred_element_type=F32)
        norm = []
        for c in range(att_w // LANES):
            sc = inv[c * heads_per_vreg]
            for e_i in range(1, heads_per_vreg):
                sc = jnp.where(lane_t < e_i * hd, sc, inv[c * heads_per_vreg + e_i])
            norm.append(sc)
        o = o * jnp.concatenate(norm, axis=1)
        ycat_ref[rows, mix_w:] = (o * gate_ref[rows, mix_w:].astype(F32)).astype(BF16)

    z = jnp.dot(jnp.concatenate(y_act, axis=1), wglu_ref[...], preferred_element_type=F32)
    y_ssm = z[:, 0:ssm_w] * _sigmoid(z[:, ssm_w:])
    for j in range(n_ssm):
        utb_ref[n_pool + j, cur_rows, :] = y_ssm[:, j * LANES:(j + 1) * LANES]

    for c in range(NB // BC):
        crow = slice(c * BC * TT, (c + 1) * BC * TT)
        for bb in range(BC):
            b = c * BC + bb
            rows = slice(b * TT, (b + 1) * TT)
            for k in range(n_slab):
                cols = slice(k * LANES, (k + 1) * LANES)
                yk = utb_ref[k, pl.ds(HALO * NB + b, TT, stride=NB), :]
                ycat_ref[rows, cols] = (yk * gate_ref[rows, cols].astype(F32)).astype(BF16)
        out = jnp.dot(ycat_ref[crow, :], wout_s[...], preferred_element_type=F32)
        ms2 = jnp.mean(out * out, axis=-1, keepdims=True)
        res = xres_ref[c * BC:(c + 1) * BC].reshape(BC * TT, d) + (out * lax.rsqrt(ms2 + EPS)) * gpost_ref[...]
        out_ref[c * BC:(c + 1) * BC] = res.reshape(BC, TT, d)
        front(c, xn_ref)


def _const_spec(shape):
    zeros = (0,) * len(shape)
    return pl.BlockSpec(shape, lambda bi, ti: zeros, pipeline_mode=pl.Buffered(1))


def _layer_call(x, kt, v, g_pre, w_in, wpool_bd, pscale, inv_w, inv_c0, wtb, wc, a_mat, d_skip, w_glu, w_out, g_post,
                *, pool_w, ssm_w, att_w, n_mem):
    bsz, seq, d = x.shape
    assert bsz % NB == 0 and seq % TT == 0 and TT % TB == 0 and TT % TC == 0 and TB == HALO and NB % BC == 0
    assert pool_w % LANES == 0 and ssm_w % LANES == 0 and att_w % LANES == 0
    n_pool, n_ssm = pool_w // LANES, ssm_w // LANES
    n_slab = n_pool + n_ssm
    n_state = a_mat.shape[-1]
    m_rows = NB * TT
    kern = functools.partial(_layer_kernel, d=d, pool_w=pool_w, ssm_w=ssm_w, att_w=att_w, n_mem=n_mem)
    consts = (g_pre, w_in, wpool_bd, pscale, inv_w, inv_c0, wtb, wc, a_mat, d_skip, w_glu, w_out, g_post)
    n_t = seq // TT
    n_tiles = (bsz // NB) * n_t

    def next_tile(bi, ti):
        nxt = jnp.minimum(bi * n_t + ti + 1, n_tiles - 1)
        return (nxt // n_t, nxt % n_t, 0)

    return pl.pallas_call(
        kern,
        grid=(bsz // NB, n_t),
        in_specs=[
            pl.BlockSpec((NB, TT, d), next_tile),
            pl.BlockSpec(memory_space=pl.ANY),
            pl.BlockSpec(memory_space=pl.ANY),
            pl.BlockSpec(memory_space=pl.ANY),
        ] + [pl.BlockSpec(memory_space=pl.ANY) if c is w_in or c is w_out else _const_spec(c.shape) for c in consts],
        out_specs=pl.BlockSpec((NB, TT, d), lambda bi, ti: (bi, ti, 0)),
        out_shape=jax.ShapeDtypeStruct(x.shape, x.dtype),
        scratch_shapes=[
            pltpu.VMEM((n_slab, (HALO + TT) * NB, LANES), F32),
            pltpu.VMEM((m_rows, d), BF16),
            pltpu.VMEM((m_rows, att_w), BF16),
            pltpu.VMEM((m_rows, d), BF16),
            pltpu.VMEM((n_ssm, m_rows // TC, n_state), F32),
            pltpu.VMEM((n_ssm, m_rows // TC, n_state), F32),
            pltpu.VMEM((n_ssm, NB, n_state), F32),
            pltpu.VMEM((m_rows, MEM_HEADS * n_mem), F32),
            pltpu.VMEM(w_in.shape, BF16),
            pltpu.VMEM(w_out.shape, BF16),
            pltpu.VMEM((NB,) + kt.shape[1:], BF16),
            pltpu.VMEM((NB,) + v.shape[1:], BF16),
            pltpu.VMEM((2, MEM_HEADS) + kt.shape[1:], BF16),
            pltpu.VMEM((NB, TT, d), F32),
        ],
        compiler_params=pltpu.CompilerParams(
            dimension_semantics=("arbitrary", "arbitrary"),
            vmem_limit_bytes=VMEM_LIMIT_BYTES,
        ),
        name="layer_call",
    )(x, x, kt, v, *consts)


def _expand_kernel(xt_ref, xb_ref, xc_ref, wtb_ref, wc_ref, *, c, n_st):
    gpb = LANES // c
    tl = TC * LANES

    def rep(rows, cols, row_blk, col_blk, period):
        r = lax.broadcasted_iota(jnp.int32, (rows, cols), 0)
        q = lax.broadcasted_iota(jnp.int32, (rows, cols), 1)
        hit = (r // row_blk == q // col_blk) & (r % period == q % period)
        return jnp.where(hit, 1.0, 0.0).astype(BF16)

    def same_group(rows, cols, row_div, col_div):
        r = lax.broadcasted_iota(jnp.int32, (rows, cols), 0)
        q = lax.broadcasted_iota(jnp.int32, (rows, cols), 1)
        return (r // row_div) % gpb == (q // col_div) % gpb

    rep_t = rep(TC * c, tl, c, LANES, c)
    rep_b = rep(2 * n_st, 2 * gpb * n_st, n_st, gpb * n_st, n_st)
    wt = jnp.dot(xt_ref[0].astype(BF16), rep_t, preferred_element_type=F32)
    wtb_ref[0, :, 0:tl] = jnp.where(same_group(tl, tl, c, c), wt, 0.0).astype(BF16)
    wb = jnp.dot(xb_ref[0].astype(BF16), rep_b, preferred_element_type=F32)
    wtb_ref[0, :, tl:] = jnp.where(same_group(tl, 2 * gpb * n_st, c, n_st), wb, 0.0).astype(BF16)
    wcx = jnp.dot(xc_ref[0].astype(BF16), rep_t, preferred_element_type=F32)
    wc_ref[0] = jnp.where(same_group(2 * gpb * n_st, tl, n_st, c), wcx, 0.0).astype(BF16)


def _ssm_matrices(a_re, a_im, log_dt, b_re, b_im, c_re, c_im):
    n_grp, n_st = a_re.shape
    c = b_re.shape[-1]
    gpb = LANES // c
    nj = n_grp // gpb
    hi = lax.Precision.HIGHEST
    dt = jnp.exp(log_dt)[:, None]
    ks = jnp.arange(TC + 1, dtype=F32)[:, None, None]
    mag = jnp.exp(ks * (a_re * dt)[None])
    ang = ks * (a_im * dt)[None]
    p_re, p_im = mag * jnp.cos(ang), mag * jnp.sin(ang)
    den = a_re * a_re + a_im * a_im
    x_re, x_im = p_re[1] - 1.0, p_im[1]
    f_re = (x_re * a_re + x_im * a_im) / den
    f_im = (x_im * a_re - x_re * a_im) / den
    bt_re, bt_im = jnp.swapaxes(b_re, 1, 2), jnp.swapaxes(b_im, 1, 2)
    bb_re = f_re[:, None, :] * bt_re - f_im[:, None, :] * bt_im
    bb_im = f_re[:, None, :] * bt_im + f_im[:, None, :] * bt_re
    cp_re = c_re[None] * p_re[:, :, None, :] - c_im[None] * p_im[:, :, None, :]
    cp_im = c_re[None] * p_im[:, :, None, :] + c_im[None] * p_re[:, :, None, :]
    kern = (jnp.einsum('gin,kgon->kgio', bb_re, cp_re[:TC], precision=hi)
            - jnp.einsum('gin,kgon->kgio', bb_im, cp_im[:TC], precision=hi))
    zero = jnp.zeros_like(kern[0])
    xt = jnp.stack([jnp.concatenate([kern[t - s] if t >= s else zero for t in range(TC)], axis=-1)
                    for s in range(TC)], axis=0)
    xt = xt.reshape(TC, nj, gpb * c, TC * c).swapaxes(0, 1).reshape(nj, TC * LANES, TC * c)
    q_re, q_im = p_re[TC - 1::-1][:, :, None, :], p_im[TC - 1::-1][:, :, None, :]
    xb = jnp.concatenate([q_re * bb_re[None] - q_im * bb_im[None], q_re * bb_im[None] + q_im * bb_re[None]], axis=-1)
    xb = xb.reshape(TC, nj, gpb * c, 2 * n_st).swapaxes(0, 1).reshape(nj, TC * LANES, 2 * n_st)
    xc = jnp.stack([cp_re[1:], -cp_im[1:]], axis=0)
    xc = jnp.transpose(xc, (2, 0, 4, 1, 3)).reshape(nj, gpb, 2, n_st, TC * c)
    xc = xc.swapaxes(1, 2).reshape(nj, 2 * gpb * n_st, TC * c)
    a_mat = jnp.concatenate([p_re[TC].reshape(nj, gpb * n_st), p_im[TC].reshape(nj, gpb * n_st)], axis=-1)
    a_mat = jnp.broadcast_to(a_mat[:, None, :], (nj, NB, a_mat.shape[-1]))
    n_state = 2 * gpb * n_st
    wtb, wc = pl.pallas_call(
        functools.partial(_expand_kernel, c=c, n_st=n_st),
        grid=(nj,),
        in_specs=[pl.BlockSpec((1,) + xt.shape[1:], lambda j: (j, 0, 0)),
                  pl.BlockSpec((1,) + xb.shape[1:], lambda j: (j, 0, 0)),
                  pl.BlockSpec((1,) + xc.shape[1:], lambda j: (j, 0, 0))],
        out_specs=[pl.BlockSpec((1, TC * LANES, TC * LANES + n_state), lambda j: (j, 0, 0)),
                   pl.BlockSpec((1, n_state, TC * LANES), lambda j: (j, 0, 0))],
        out_shape=[jax.ShapeDtypeStruct((nj, TC * LANES, TC * LANES + n_state), BF16),
                   jax.ShapeDtypeStruct((nj, n_state, TC * LANES), BF16)],
        compiler_params=pltpu.CompilerParams(dimension_semantics=("arbitrary",)),
        name="ssm_expand_call",
    )(xt, xb, xc)
    return wtb, wc, a_mat


def _pool_constants(w_pool, pool_w):
    n_grp, gw, _ = w_pool.shape
    wbd = jnp.zeros((pool_w, pool_w), F32)
    for gi in range(n_grp):
        wbd = wbd.at[gi * gw:(gi + 1) * gw, gi * gw:(gi + 1) * gw].set(w_pool[gi])
    win = jnp.repeat(jnp.asarray(POOL_WINDOWS, F32), gw)
    inv_w = (1.0 / win)[None, :]
    pos = jnp.repeat(jnp.arange(1, HALO + 1, dtype=F32), NB)[:, None]
    inv_c0 = 1.0 / jnp.minimum(pos, win[None, :])
    return wbd.astype(BF16), inv_w, inv_c0


def _layer(x, mem, g_pre, w_in, w_pool, pool_scale, a_re, a_im, log_dt, b_re, b_im, c_re, c_im, d_skip, w_glu,
           g_mem, w_kv, w_out, g_post):
    d = x.shape[-1]
    n_mem = mem.shape[1]
    pool_w = w_pool.shape[0] * w_pool.shape[1]
    ssm_w = d_skip.shape[0]
    att_w = w_kv.shape[1] // 2
    assert a_re.shape[0] * SSM_GROUP == ssm_w and b_re.shape[-1] == SSM_GROUP
    assert pool_w + ssm_w + att_w == d and w_pool.shape[0] == len(POOL_WINDOWS)

    kt, v = _kv_call(mem, g_mem[None, :], w_kv[:, :att_w].T.astype(BF16), w_kv[:, att_w:].astype(BF16))
    wtb, wc, a_mat = _ssm_matrices(a_re, a_im, log_dt, b_re, b_im, c_re, c_im)
    wpool_bd, inv_w, inv_c0 = _pool_constants(w_pool, pool_w)
    return _layer_call(
        x, kt, v, g_pre[None, :], w_in.astype(BF16), wpool_bd, pool_scale[None, :], inv_w, inv_c0,
        wtb, wc, a_mat, d_skip[None, :], w_glu.astype(BF16), w_out.astype(BF16), g_post[None, :],
        pool_w=pool_w, ssm_w=ssm_w, att_w=att_w, n_mem=n_mem)


def kernel(x, mem, g_pre, w_in, w_pool, pool_scale, a_re, a_im, log_dt, b_re, b_im, c_re, c_im, d_skip, w_glu, g_mem, w_kv, w_out, g_post):
    for i in range(g_pre.shape[0]):
        x = _layer(x, mem, g_pre[i], w_in[i], w_pool[i], pool_scale[i], a_re[i], a_im[i], log_dt[i], b_re[i], b_im[i],
                   c_re[i], c_im[i], d_skip[i], w_glu[i], g_mem[i], w_kv[i], w_out[i], g_post[i])
    return x
```

```python
import functools

import jax
import jax.numpy as jnp
from jax import lax
from jax.experimental import pallas as pl
from jax.experimental.pallas import tpu as pltpu

LANES = 128
NB = 8
TT = 128
TC = 4
BC = 4
TB = 16
KV_MB = 4
HALO = 16
POOL_WINDOWS = (2, 4, 8, 16)
SSM_GROUP = 16
MEM_HEADS = 4
EPS = 1e-6
LOG2_E = 1.4426950408889634
GELU_C = 0.7978845608028654
VMEM_LIMIT_BYTES = 60 * 1024 * 1024

F32 = jnp.float32
BF16 = jnp.bfloat16


def _sigmoid(v):
    return 0.5 * jnp.tanh(0.5 * v) + 0.5


def _silu(v):
    hv = 0.5 * v
    return hv + hv * jnp.tanh(hv)


def _gelu_tanh(v):
    hv = 0.5 * v
    return hv + hv * jnp.tanh(v * (GELU_C + (GELU_C * 0.044715) * (v * v)))


def _kv_kernel(mem_ref, gmem_ref, wkt_ref, wv_ref, kbd_ref, vbd_ref, *, att_w, n_mem):
    mb, _, d = mem_ref.shape
    m = mem_ref[...].reshape(mb * n_mem, d)
    ms = jnp.mean(m * m, axis=-1, keepdims=True)
    mn = ((m * lax.rsqrt(ms + EPS)) * gmem_ref[...]).astype(BF16)
    kt = lax.dot_general(wkt_ref[...], mn, (((1,), (1,)), ((), ())), preferred_element_type=F32)
    v = jnp.dot(mn, wv_ref[...], preferred_element_type=F32)
    hd = att_w // MEM_HEADS
    kt = kt * (hd ** -0.5 * LOG2_E)
    row_head = lax.broadcasted_iota(jnp.int32, (att_w, n_mem), 0) // hd
    col_head = lax.broadcasted_iota(jnp.int32, (n_mem, att_w), 1) // hd
    for i in range(mb):
        kt_i = kt[:, i * n_mem:(i + 1) * n_mem]
        v_i = v[i * n_mem:(i + 1) * n_mem, :]
        for hh in range(MEM_HEADS):
            kbd_ref[i, :, hh * n_mem:(hh + 1) * n_mem] = jnp.where(row_head == hh, kt_i, 0.0).astype(BF16)
            vbd_ref[i, hh * n_mem:(hh + 1) * n_mem, :] = jnp.where(col_head == hh, v_i, 0.0).astype(BF16)


def _kv_call(mem, g_mem, wkt, wv):
    bsz, n_mem, d = mem.shape
    att_w = wv.shape[1]
    assert bsz % KV_MB == 0
    kern = functools.partial(_kv_kernel, att_w=att_w, n_mem=n_mem)
    return pl.pallas_call(
        kern,
        grid=(bsz // KV_MB,),
        in_specs=[
            pl.BlockSpec((KV_MB, n_mem, d), lambda b: (b, 0, 0)),
            pl.BlockSpec((1, d), lambda b: (0, 0)),
            pl.BlockSpec((att_w, d), lambda b: (0, 0)),
            pl.BlockSpec((d, att_w), lambda b: (0, 0)),
        ],
        out_specs=[
            pl.BlockSpec((KV_MB, att_w, MEM_HEADS * n_mem), lambda b: (b, 0, 0)),
            pl.BlockSpec((KV_MB, MEM_HEADS * n_mem, att_w), lambda b: (b, 0, 0)),
        ],
        out_shape=[
            jax.ShapeDtypeStruct((bsz, att_w, MEM_HEADS * n_mem), BF16),
            jax.ShapeDtypeStruct((bsz, MEM_HEADS * n_mem, att_w), BF16),
        ],
        compiler_params=pltpu.CompilerParams(dimension_semantics=("arbitrary",)),
        name="kv_call",
    )(mem, g_mem, wkt, wv)


def _layer_kernel(x_ref, kbd_ref, vbd_ref, gpre_ref, win_ref, wpool_ref, pscale_ref, invw_ref, invc0_ref,
                  wtb_ref, wc_ref, a_ref, dskip_ref, wglu_ref, wout_ref, gpost_ref,
                  out_ref,
                  utb_ref, gate_ref, q_ref, ycat_ref, pd_ref, g_ref, hin_ref, hc_ref,
                  s_ref, win_s, wout_s, kbd_s,
                  *, d, pool_w, ssm_w, att_w, n_mem):
    bi = pl.program_id(0)
    ti = pl.program_id(1)
    n_pool = pool_w // LANES
    n_ssm = ssm_w // LANES
    n_slab = n_pool + n_ssm
    mix_w = pool_w + ssm_w
    n_chunk = TT // TC
    st = hc_ref.shape[-1] // 2
    hd = att_w // MEM_HEADS
    cur_rows = slice(HALO * NB, (HALO + TT) * NB)

    @pl.when((bi == 0) & (ti == 0))
    def _():
        pltpu.sync_copy(win_ref, win_s)
        pltpu.sync_copy(wout_ref, wout_s)

    @pl.when(ti == 0)
    def _():
        pltpu.sync_copy(kbd_ref.at[pl.ds(pl.multiple_of(bi * NB, NB), NB)], kbd_s)
        utb_ref[0:n_pool, 0:HALO * NB, :] = jnp.zeros((n_pool, HALO * NB, LANES), F32)
        hc_ref[...] = jnp.zeros(hc_ref.shape, F32)

    for c in range(NB // BC):
        crow = slice(c * BC * TT, (c + 1) * BC * TT)
        x = x_ref[c * BC:(c + 1) * BC].reshape(BC * TT, d)
        ms = jnp.mean(x * x, axis=-1, keepdims=True)
        h = ((x * lax.rsqrt(ms + EPS)) * gpre_ref[...]).astype(BF16)
        val = jnp.dot(h, win_s[:, 0:mix_w], preferred_element_type=F32)
        for bb in range(BC):
            for k in range(n_slab):
                utb_ref[k, pl.ds(HALO * NB + c * BC + bb, TT, stride=NB), :] = (
                    val[bb * TT:(bb + 1) * TT, k * LANES:(k + 1) * LANES])
        q_ref[crow, :] = jnp.dot(h, win_s[:, mix_w:mix_w + att_w], preferred_element_type=F32).astype(BF16)
        gate = jnp.dot(h, win_s[:, mix_w + att_w:], preferred_element_type=F32)
        gate_ref[crow, :] = _silu(gate).astype(BF16)

    y_intra = []
    for j in range(n_ssm):
        u = utb_ref[n_pool + j, cur_rows, :]
        u4 = u.reshape(n_chunk, TC, NB, LANES)
        up = jnp.concatenate([u4[:, s].reshape(n_chunk * NB, LANES) for s in range(TC)], axis=1).astype(BF16)
        yg = jnp.dot(up, wtb_ref[j], preferred_element_type=F32)
        y_intra.append(yg[:, 0:TC * LANES])
        g_ref[j] = yg[:, TC * LANES:]

    for b in range(NB):
        rows = slice(b * TT, (b + 1) * TT)
        s_ref[rows, :] = jnp.dot(q_ref[rows, :], kbd_s[b], preferred_element_type=F32)

    gw = pool_w // len(POOL_WINDOWS)
    lane = lax.broadcasted_iota(jnp.int32, (TB * NB, LANES), 1)
    for k in range(n_pool):
        w_lo = POOL_WINDOWS[(k * LANES) // gw]
        w_hi = POOL_WINDOWS[min(((k + 1) * LANES - 1) // gw, len(POOL_WINDOWS) - 1)]
        split = ((k * LANES) // gw + 1) * gw - k * LANES
        for i in range(TT // TB):
            base = (HALO + i * TB) * NB
            ext = utb_ref[k, base - (HALO - 1) * NB: base + TB * NB, :]
            sums = {1: ext}
            w = 1
            while w < w_hi:
                prev = sums[w]
                sums[2 * w] = prev[w * NB:] + prev[:-w * NB]
                w *= 2
            cur = ext[(HALO - 1) * NB:]
            lo = sums[w_lo][(HALO - w_lo) * NB:]
            hi = sums[w_hi][(HALO - w_hi) * NB:]
            ssum = jnp.where(lane < split, lo, hi)
            scale = invw_ref[:, k * LANES:(k + 1) * LANES]
            if i == 0:
                scale = jnp.where(ti == 0, invc0_ref[:, k * LANES:(k + 1) * LANES], scale)
            pd_ref[i * TB * NB:(i + 1) * TB * NB, k * LANES:(k + 1) * LANES] = (ssum * scale - cur).astype(BF16)
    for k in range(n_pool):
        utb_ref[k, 0:HALO * NB, :] = utb_ref[k, TT * NB:(TT + HALO) * NB, :]
    ypool = jnp.dot(pd_ref[...], wpool_ref[...], preferred_element_type=F32) * pscale_ref[...]
    for k in range(n_pool):
        utb_ref[k, cur_rows, :] = ypool[:, k * LANES:(k + 1) * LANES]

    a_re = [a_ref[j, :, 0:st] for j in range(n_ssm)]
    a_im = [a_ref[j, :, st:] for j in range(n_ssm)]
    h_re = [hc_ref[j, :, 0:st] for j in range(n_ssm)]
    h_im = [hc_ref[j, :, st:] for j in range(n_ssm)]
    for r in range(n_chunk):
        rows = slice(r * NB, (r + 1) * NB)
        for j in range(n_ssm):
            hin_ref[j, rows, 0:st] = h_re[j]
            hin_ref[j, rows, st:] = h_im[j]
            g_re = g_ref[j, rows, 0:st]
            g_im = g_ref[j, rows, st:]
            h_re[j], h_im[j] = (a_re[j] * h_re[j] - a_im[j] * h_im[j] + g_re,
                                a_re[j] * h_im[j] + a_im[j] * h_re[j] + g_im)
    for j in range(n_ssm):
        hc_ref[j, :, 0:st] = h_re[j]
        hc_ref[j, :, st:] = h_im[j]

    y_act = []
    for j in range(n_ssm):
        y_lin = y_intra[j] + jnp.dot(hin_ref[j].astype(BF16), wc_ref[j], preferred_element_type=F32)
        y_tb = jnp.stack([y_lin[:, s * LANES:(s + 1) * LANES].reshape(n_chunk, NB, LANES) for s in range(TC)],
                         axis=1).reshape(TT * NB, LANES)
        y = y_tb + dskip_ref[:, j * LANES:(j + 1) * LANES] * utb_ref[n_pool + j, cur_rows, :]
        y_act.append(_gelu_tanh(y).astype(BF16))

    lane_t = lax.broadcasted_iota(jnp.int32, (TT, LANES), 1)
    heads_per_vreg = LANES // hd

    def attend(b):
        rows = slice(b * TT, (b + 1) * TT)
        es, inv = [], []
        for hh in range(MEM_HEADS):
            sh = s_ref[rows, hh * n_mem:(hh + 1) * n_mem]
            e = jnp.exp2(sh - jnp.max(sh, axis=-1, keepdims=True))
            inv.append(1.0 / jnp.sum(e, axis=-1, keepdims=True))
            es.append(e.astype(BF16))
        o = jnp.dot(jnp.concatenate(es, axis=1), vbd_ref[b], preferred_element_type=F32)
        norm = []
        for c in range(att_w // LANES):
            sc = inv[c * heads_per_vreg]
            for e_i in range(1, heads_per_vreg):
                sc = jnp.where(lane_t < e_i * hd, sc, inv[c * heads_per_vreg + e_i])
            norm.append(sc)
        o = o * jnp.concatenate(norm, axis=1)
        ycat_ref[rows, mix_w:] = (o * gate_ref[rows, mix_w:].astype(F32)).astype(BF16)

    for b in range(NB // 2):
        attend(b)

    z = jnp.dot(jnp.concatenate(y_act, axis=1), wglu_ref[...], preferred_element_type=F32)
    y_ssm = z[:, 0:ssm_w] * _sigmoid(z[:, ssm_w:])
    for j in range(n_ssm):
        utb_ref[n_pool + j, cur_rows, :] = y_ssm[:, j * LANES:(j + 1) * LANES]

    for b in range(NB // 2, NB):
        attend(b)

    for c in range(NB // BC):
        crow = slice(c * BC * TT, (c + 1) * BC * TT)
        for bb in range(BC):
            b = c * BC + bb
            rows = slice(b * TT, (b + 1) * TT)
            for k in range(n_slab):
                cols = slice(k * LANES, (k + 1) * LANES)
                yk = utb_ref[k, pl.ds(HALO * NB + b, TT, stride=NB), :]
                ycat_ref[rows, cols] = (yk * gate_ref[rows, cols].astype(F32)).astype(BF16)
        out = jnp.dot(ycat_ref[crow, :], wout_s[...], preferred_element_type=F32)
        ms2 = jnp.mean(out * out, axis=-1, keepdims=True)
        res = x_ref[c * BC:(c + 1) * BC].reshape(BC * TT, d) + (out * lax.rsqrt(ms2 + EPS)) * gpost_ref[...]
        out_ref[c * BC:(c + 1) * BC] = res.reshape(BC, TT, d)


def _const_spec(shape):
    zeros = (0,) * len(shape)
    return pl.BlockSpec(shape, lambda bi, ti: zeros, pipeline_mode=pl.Buffered(1))


def _layer_call(x, kbd, vbd, g_pre, w_in, wpool_bd, pscale, inv_w, inv_c0, wtb, wc, a_mat, d_skip, w_glu, w_out, g_post,
                *, pool_w, ssm_w, att_w, n_mem):
    bsz, seq, d = x.shape
    assert bsz % NB == 0 and seq % TT == 0 and TT % TB == 0 and TT % TC == 0 and TB == HALO and NB % BC == 0
    assert pool_w % LANES == 0 and ssm_w % LANES == 0 and att_w % LANES == 0
    n_pool, n_ssm = pool_w // LANES, ssm_w // LANES
    n_slab = n_pool + n_ssm
    n_state = a_mat.shape[-1]
    m_rows = NB * TT
    kern = functools.partial(_layer_kernel, d=d, pool_w=pool_w, ssm_w=ssm_w, att_w=att_w, n_mem=n_mem)
    consts = (g_pre, w_in, wpool_bd, pscale, inv_w, inv_c0, wtb, wc, a_mat, d_skip, w_glu, w_out, g_post)
    return pl.pallas_call(
        kern,
        grid=(bsz // NB, seq // TT),
        in_specs=[
            pl.BlockSpec((NB, TT, d), lambda bi, ti: (bi, ti, 0)),
            pl.BlockSpec(memory_space=pl.ANY),
            pl.BlockSpec((NB,) + vbd.shape[1:], lambda bi, ti: (bi, 0, 0), pipeline_mode=pl.Buffered(1)),
        ] + [pl.BlockSpec(memory_space=pl.ANY) if c is w_in or c is w_out else _const_spec(c.shape) for c in consts],
        out_specs=pl.BlockSpec((NB, TT, d), lambda bi, ti: (bi, ti, 0)),
        out_shape=jax.ShapeDtypeStruct(x.shape, x.dtype),
        scratch_shapes=[
            pltpu.VMEM((n_slab, (HALO + TT) * NB, LANES), F32),
            pltpu.VMEM((m_rows, d), BF16),
            pltpu.VMEM((m_rows, att_w), BF16),
            pltpu.VMEM((m_rows, d), BF16),
            pltpu.VMEM((m_rows, pool_w), BF16),
            pltpu.VMEM((n_ssm, m_rows // TC, n_state), F32),
            pltpu.VMEM((n_ssm, m_rows // TC, n_state), F32),
            pltpu.VMEM((n_ssm, NB, n_state), F32),
            pltpu.VMEM((m_rows, MEM_HEADS * n_mem), F32),
            pltpu.VMEM(w_in.shape, BF16),
            pltpu.VMEM(w_out.shape, BF16),
            pltpu.VMEM((NB,) + kbd.shape[1:], BF16),
        ],
        compiler_params=pltpu.CompilerParams(
            dimension_semantics=("arbitrary", "arbitrary"),
            vmem_limit_bytes=VMEM_LIMIT_BYTES,
        ),
        name="layer_call",
    )(x, kbd, vbd, *consts)


def _expand_kernel(xt_ref, xb_ref, xc_ref, wtb_ref, wc_ref, *, c, n_st):
    gpb = LANES // c
    tl = TC * LANES

    def rep(rows, cols, row_blk, col_blk, period):
        r = lax.broadcasted_iota(jnp.int32, (rows, cols), 0)
        q = lax.broadcasted_iota(jnp.int32, (rows, cols), 1)
        hit = (r // row_blk == q // col_blk) & (r % period == q % period)
        return jnp.where(hit, 1.0, 0.0).astype(BF16)

    def same_group(rows, cols, row_div, col_div):
        r = lax.broadcasted_iota(jnp.int32, (rows, cols), 0)
        q = lax.broadcasted_iota(jnp.int32, (rows, cols), 1)
        return (r // row_div) % gpb == (q // col_div) % gpb

    rep_t = rep(TC * c, tl, c, LANES, c)
    rep_b = rep(2 * n_st, 2 * gpb * n_st, n_st, gpb * n_st, n_st)
    wt = jnp.dot(xt_ref[0].astype(BF16), rep_t, preferred_element_type=F32)
    wtb_ref[0, :, 0:tl] = jnp.where(same_group(tl, tl, c, c), wt, 0.0).astype(BF16)
    wb = jnp.dot(xb_ref[0].astype(BF16), rep_b, preferred_element_type=F32)
    wtb_ref[0, :, tl:] = jnp.where(same_group(tl, 2 * gpb * n_st, c, n_st), wb, 0.0).astype(BF16)
    wcx = jnp.dot(xc_ref[0].astype(BF16), rep_t, preferred_element_type=F32)
    wc_ref[0] = jnp.where(same_group(2 * gpb * n_st, tl, n_st, c), wcx, 0.0).astype(BF16)


def _ssm_matrices(a_re, a_im, log_dt, b_re, b_im, c_re, c_im):
    n_grp, n_st = a_re.shape
    c = b_re.shape[-1]
    gpb = LANES // c
    nj = n_grp // gpb
    hi = lax.Precision.HIGHEST
    dt = jnp.exp(log_dt)[:, None]
    ks = jnp.arange(TC + 1, dtype=F32)[:, None, None]
    mag = jnp.exp(ks * (a_re * dt)[None])
    ang = ks * (a_im * dt)[None]
    p_re, p_im = mag * jnp.cos(ang), mag * jnp.sin(ang)
    den = a_re * a_re + a_im * a_im
    x_re, x_im = p_re[1] - 1.0, p_im[1]
    f_re = (x_re * a_re + x_im * a_im) / den
    f_im = (x_im * a_re - x_re * a_im) / den
    bt_re, bt_im = jnp.swapaxes(b_re, 1, 2), jnp.swapaxes(b_im, 1, 2)
    bb_re = f_re[:, None, :] * bt_re - f_im[:, None, :] * bt_im
    bb_im = f_re[:, None, :] * bt_im + f_im[:, None, :] * bt_re
    cp_re = c_re[None] * p_re[:, :, None, :] - c_im[None] * p_im[:, :, None, :]
    cp_im = c_re[None] * p_im[:, :, None, :] + c_im[None] * p_re[:, :, None, :]
    kern = (jnp.einsum('gin,kgon->kgio', bb_re, cp_re[:TC], precision=hi)
            - jnp.einsum('gin,kgon->kgio', bb_im, cp_im[:TC], precision=hi))
    zero = jnp.zeros_like(kern[0])
    xt = jnp.stack([jnp.concatenate([kern[t - s] if t >= s else zero for t in range(TC)], axis=-1)
                    for s in range(TC)], axis=0)
    xt = xt.reshape(TC, nj, gpb * c, TC * c).swapaxes(0, 1).reshape(nj, TC * LANES, TC * c)
    q_re, q_im = p_re[TC - 1::-1][:, :, None, :], p_im[TC - 1::-1][:, :, None, :]
    xb = jnp.concatenate([q_re * bb_re[None] - q_im * bb_im[None], q_re * bb_im[None] + q_im * bb_re[None]], axis=-1)
    xb = xb.reshape(TC, nj, gpb * c, 2 * n_st).swapaxes(0, 1).reshape(nj, TC * LANES, 2 * n_st)
    xc = jnp.stack([cp_re[1:], -cp_im[1:]], axis=0)
    xc = jnp.transpose(xc, (2, 0, 4, 1, 3)).reshape(nj, gpb, 2, n_st, TC * c)
    xc = xc.swapaxes(1, 2).reshape(nj, 2 * gpb * n_st, TC * c)
    a_mat = jnp.concatenate([p_re[TC].reshape(nj, gpb * n_st), p_im[TC].reshape(nj, gpb * n_st)], axis=-1)
    a_mat = jnp.broadcast_to(a_mat[:, None, :], (nj, NB, a_mat.shape[-1]))
    n_state = 2 * gpb * n_st
    wtb, wc = pl.pallas_call(
        functools.partial(_expand_kernel, c=c, n_st=n_st),
        grid=(nj,),
        in_specs=[pl.BlockSpec((1,) + xt.shape[1:], lambda j: (j, 0, 0)),
                  pl.BlockSpec((1,) + xb.shape[1:], lambda j: (j, 0, 0)),
                  pl.BlockSpec((1,) + xc.shape[1:], lambda j: (j, 0, 0))],
        out_specs=[pl.BlockSpec((1, TC * LANES, TC * LANES + n_state), lambda j: (j, 0, 0)),
                   pl.BlockSpec((1, n_state, TC * LANES), lambda j: (j, 0, 0))],
        out_shape=[jax.ShapeDtypeStruct((nj, TC * LANES, TC * LANES + n_state), BF16),
                   jax.ShapeDtypeStruct((nj, n_state, TC * LANES), BF16)],
        compiler_params=pltpu.CompilerParams(dimension_semantics=("arbitrary",)),
        name="ssm_expand_call",
    )(xt, xb, xc)
    return wtb, wc, a_mat


def _pool_constants(w_pool, pool_w):
    n_grp, gw, _ = w_pool.shape
    wbd = jnp.zeros((pool_w, pool_w), F32)
    for gi in range(n_grp):
        wbd = wbd.at[gi * gw:(gi + 1) * gw, gi * gw:(gi + 1) * gw].set(w_pool[gi])
    win = jnp.repeat(jnp.asarray(POOL_WINDOWS, F32), gw)
    inv_w = (1.0 / win)[None, :]
    pos = jnp.repeat(jnp.arange(1, HALO + 1, dtype=F32), NB)[:, None]
    inv_c0 = 1.0 / jnp.minimum(pos, win[None, :])
    return wbd.astype(BF16), inv_w, inv_c0


def _layer(x, mem, g_pre, w_in, w_pool, pool_scale, a_re, a_im, log_dt, b_re, b_im, c_re, c_im, d_skip, w_glu,
           g_mem, w_kv, w_out, g_post):
    d = x.shape[-1]
    n_mem = mem.shape[1]
    pool_w = w_pool.shape[0] * w_pool.shape[1]
    ssm_w = d_skip.shape[0]
    att_w = w_kv.shape[1] // 2
    assert a_re.shape[0] * SSM_GROUP == ssm_w and b_re.shape[-1] == SSM_GROUP
    assert pool_w + ssm_w + att_w == d and w_pool.shape[0] == len(POOL_WINDOWS)

    kbd, vbd = _kv_call(mem, g_mem[None, :], w_kv[:, :att_w].T.astype(BF16), w_kv[:, att_w:].astype(BF16))
    wtb, wc, a_mat = _ssm_matrices(a_re, a_im, log_dt, b_re, b_im, c_re, c_im)
    wpool_bd, inv_w, inv_c0 = _pool_constants(w_pool, pool_w)
    return _layer_call(
        x, kbd, vbd, g_pre[None, :], w_in.astype(BF16), wpool_bd, pool_scale[None, :], inv_w, inv_c0,
        wtb, wc, a_mat, d_skip[None, :], w_glu.astype(BF16), w_out.astype(BF16), g_post[None, :],
        pool_w=pool_w, ssm_w=ssm_w, att_w=att_w, n_mem=n_mem)


def kernel(x, mem, g_pre, w_in, w_pool, pool_scale, a_re, a_im, log_dt, b_re, b_im, c_re, c_im, d_skip, w_glu, g_mem, w_kv, w_out, g_post):
    for i in range(g_pre.shape[0]):
        x = _layer(x, mem, g_pre[i], w_in[i], w_pool[i], pool_scale[i], a_re[i], a_im[i], log_dt[i], b_re[i], b_im[i],
                   c_re[i], c_im[i], d_skip[i], w_glu[i], g_mem[i], w_kv[i], w_out[i], g_post[i])
    return x
```

```python
import functools

import jax
import jax.numpy as jnp
from jax import lax
from jax.experimental import pallas as pl
from jax.experimental.pallas import tpu as pltpu

LANES = 128
NB = 8
TT = 128
TC = 4
BC = 4
TB = 16
KV_MB = 4
HALO = 16
POOL_WINDOWS = (2, 4, 8, 16)
SSM_GROUP = 16
MEM_HEADS = 4
EPS = 1e-6
LOG2_E = 1.4426950408889634
GELU_C = 0.7978845608028654
VMEM_LIMIT_BYTES = 60 * 1024 * 1024

F32 = jnp.float32
BF16 = jnp.bfloat16


def _sigmoid(v):
    return 0.5 * jnp.tanh(0.5 * v) + 0.5


def _silu(v):
    hv = 0.5 * v
    return hv + hv * jnp.tanh(hv)


def _gelu_tanh(v):
    hv = 0.5 * v
    return hv + hv * jnp.tanh(v * (GELU_C + (GELU_C * 0.044715) * (v * v)))


def _kv_kernel(mem_ref, gmem_ref, wkt_ref, wv_ref, kt_ref, v_ref, *, att_w, n_mem):
    mb, _, d = mem_ref.shape
    m = mem_ref[...].reshape(mb * n_mem, d)
    ms = jnp.mean(m * m, axis=-1, keepdims=True)
    mn = ((m * lax.rsqrt(ms + EPS)) * gmem_ref[...]).astype(BF16)
    kt = lax.dot_general(wkt_ref[...], mn, (((1,), (1,)), ((), ())), preferred_element_type=F32)
    v = jnp.dot(mn, wv_ref[...], preferred_element_type=F32)
    hd = att_w // MEM_HEADS
    kt = kt * (hd ** -0.5 * LOG2_E)
    for i in range(mb):
        kt_ref[i] = kt[:, i * n_mem:(i + 1) * n_mem].astype(BF16)
        v_ref[i] = v[i * n_mem:(i + 1) * n_mem, :].astype(BF16)


def _kv_call(mem, g_mem, wkt, wv):
    bsz, n_mem, d = mem.shape
    att_w = wv.shape[1]
    assert bsz % KV_MB == 0
    kern = functools.partial(_kv_kernel, att_w=att_w, n_mem=n_mem)
    return pl.pallas_call(
        kern,
        grid=(bsz // KV_MB,),
        in_specs=[
            pl.BlockSpec((KV_MB, n_mem, d), lambda b: (b, 0, 0)),
            pl.BlockSpec((1, d), lambda b: (0, 0)),
            pl.BlockSpec((att_w, d), lambda b: (0, 0)),
            pl.BlockSpec((d, att_w), lambda b: (0, 0)),
        ],
        out_specs=[
            pl.BlockSpec((KV_MB, att_w, n_mem), lambda b: (b, 0, 0)),
            pl.BlockSpec((KV_MB, n_mem, att_w), lambda b: (b, 0, 0)),
        ],
        out_shape=[
            jax.ShapeDtypeStruct((bsz, att_w, n_mem), BF16),
            jax.ShapeDtypeStruct((bsz, n_mem, att_w), BF16),
        ],
        compiler_params=pltpu.CompilerParams(dimension_semantics=("arbitrary",)),
        name="kv_call",
    )(mem, g_mem, wkt, wv)


def _layer_kernel(x_ref, kt_ref, v_ref, gpre_ref, win_ref, wpool_ref, pscale_ref, invw_ref, invc0_ref,
                  wtb_ref, wc_ref, a_ref, dskip_ref, wglu_ref, wout_ref, gpost_ref,
                  out_ref,
                  utb_ref, gate_ref, q_ref, ycat_ref, pd_ref, g_ref, hin_ref, hc_ref,
                  s_ref, win_s, wout_s, kbd_s, vbd_s, kv_stage,
                  *, d, pool_w, ssm_w, att_w, n_mem):
    bi = pl.program_id(0)
    ti = pl.program_id(1)
    n_pool = pool_w // LANES
    n_ssm = ssm_w // LANES
    n_slab = n_pool + n_ssm
    mix_w = pool_w + ssm_w
    n_chunk = TT // TC
    st = hc_ref.shape[-1] // 2
    hd = att_w // MEM_HEADS
    cur_rows = slice(HALO * NB, (HALO + TT) * NB)

    @pl.when((bi == 0) & (ti == 0))
    def _():
        pltpu.sync_copy(win_ref, win_s)
        pltpu.sync_copy(wout_ref, wout_s)

    @pl.when(ti == 0)
    def _():
        group = pl.ds(pl.multiple_of(bi * NB, NB), NB)
        row_head = lax.broadcasted_iota(jnp.int32, (att_w, n_mem), 0) // hd
        col_head = lax.broadcasted_iota(jnp.int32, (n_mem, att_w), 1) // hd
        pltpu.sync_copy(kt_ref.at[group], kv_stage)
        for b in range(NB):
            kt_b = kv_stage[b].astype(F32)
            for hh in range(MEM_HEADS):
                kbd_s[b, :, hh * n_mem:(hh + 1) * n_mem] = jnp.where(row_head == hh, kt_b, 0.0).astype(BF16)
        pltpu.sync_copy(v_ref.at[group], kv_stage)
        for b in range(NB):
            v_b = kv_stage[b].astype(F32)
            for hh in range(MEM_HEADS):
                vbd_s[b, hh * n_mem:(hh + 1) * n_mem, :] = jnp.where(col_head == hh, v_b, 0.0).astype(BF16)
        utb_ref[0:n_pool, 0:HALO * NB, :] = jnp.zeros((n_pool, HALO * NB, LANES), F32)
        hc_ref[...] = jnp.zeros(hc_ref.shape, F32)

    for c in range(NB // BC):
        crow = slice(c * BC * TT, (c + 1) * BC * TT)
        x = x_ref[c * BC:(c + 1) * BC].reshape(BC * TT, d)
        ms = jnp.mean(x * x, axis=-1, keepdims=True)
        h = ((x * lax.rsqrt(ms + EPS)) * gpre_ref[...]).astype(BF16)
        val = jnp.dot(h, win_s[:, 0:mix_w], preferred_element_type=F32)
        for bb in range(BC):
            for k in range(n_slab):
                utb_ref[k, pl.ds(HALO * NB + c * BC + bb, TT, stride=NB), :] = (
                    val[bb * TT:(bb + 1) * TT, k * LANES:(k + 1) * LANES])
        q_ref[crow, :] = jnp.dot(h, win_s[:, mix_w:mix_w + att_w], preferred_element_type=F32).astype(BF16)
        gate = jnp.dot(h, win_s[:, mix_w + att_w:], preferred_element_type=F32)
        gate_ref[crow, :] = _silu(gate).astype(BF16)

    y_intra = []
    for j in range(n_ssm):
        u = utb_ref[n_pool + j, cur_rows, :]
        u4 = u.reshape(n_chunk, TC, NB, LANES)
        up = jnp.concatenate([u4[:, s].reshape(n_chunk * NB, LANES) for s in range(TC)], axis=1).astype(BF16)
        yg = jnp.dot(up, wtb_ref[j], preferred_element_type=F32)
        y_intra.append(yg[:, 0:TC * LANES])
        g_ref[j] = yg[:, TC * LANES:]

    for b in range(NB):
        rows = slice(b * TT, (b + 1) * TT)
        s_ref[rows, :] = jnp.dot(q_ref[rows, :], kbd_s[b], preferred_element_type=F32)

    gw = pool_w // len(POOL_WINDOWS)
    lane = lax.broadcasted_iota(jnp.int32, (TB * NB, LANES), 1)
    for k in range(n_pool):
        w_lo = POOL_WINDOWS[(k * LANES) // gw]
        w_hi = POOL_WINDOWS[min(((k + 1) * LANES - 1) // gw, len(POOL_WINDOWS) - 1)]
        split = ((k * LANES) // gw + 1) * gw - k * LANES
        for i in range(TT // TB):
            base = (HALO + i * TB) * NB
            ext = utb_ref[k, base - (HALO - 1) * NB: base + TB * NB, :]
            sums = {1: ext}
            w = 1
            while w < w_hi:
                prev = sums[w]
                sums[2 * w] = prev[w * NB:] + prev[:-w * NB]
                w *= 2
            cur = ext[(HALO - 1) * NB:]
            lo = sums[w_lo][(HALO - w_lo) * NB:]
            hi = sums[w_hi][(HALO - w_hi) * NB:]
            ssum = jnp.where(lane < split, lo, hi)
            scale = invw_ref[:, k * LANES:(k + 1) * LANES]
            if i == 0:
                scale = jnp.where(ti == 0, invc0_ref[:, k * LANES:(k + 1) * LANES], scale)
            pd_ref[i * TB * NB:(i + 1) * TB * NB, k * LANES:(k + 1) * LANES] = (ssum * scale - cur).astype(BF16)
    for k in range(n_pool):
        utb_ref[k, 0:HALO * NB, :] = utb_ref[k, TT * NB:(TT + HALO) * NB, :]
    ypool = jnp.dot(pd_ref[...], wpool_ref[...], preferred_element_type=F32) * pscale_ref[...]
    for k in range(n_pool):
        utb_ref[k, cur_rows, :] = ypool[:, k * LANES:(k + 1) * LANES]

    a_re = [a_ref[j, :, 0:st] for j in range(n_ssm)]
    a_im = [a_ref[j, :, st:] for j in range(n_ssm)]
    h_re = [hc_ref[j, :, 0:st] for j in range(n_ssm)]
    h_im = [hc_ref[j, :, st:] for j in range(n_ssm)]
    for r in range(n_chunk):
        rows = slice(r * NB, (r + 1) * NB)
        for j in range(n_ssm):
            hin_ref[j, rows, 0:st] = h_re[j]
            hin_ref[j, rows, st:] = h_im[j]
            g_re = g_ref[j, rows, 0:st]
            g_im = g_ref[j, rows, st:]
            h_re[j], h_im[j] = (a_re[j] * h_re[j] - a_im[j] * h_im[j] + g_re,
                                a_re[j] * h_im[j] + a_im[j] * h_re[j] + g_im)
    for j in range(n_ssm):
        hc_ref[j, :, 0:st] = h_re[j]
        hc_ref[j, :, st:] = h_im[j]

    y_act = []
    for j in range(n_ssm):
        y_lin = y_intra[j] + jnp.dot(hin_ref[j].astype(BF16), wc_ref[j], preferred_element_type=F32)
        y_tb = jnp.stack([y_lin[:, s * LANES:(s + 1) * LANES].reshape(n_chunk, NB, LANES) for s in range(TC)],
                         axis=1).reshape(TT * NB, LANES)
        y = y_tb + dskip_ref[:, j * LANES:(j + 1) * LANES] * utb_ref[n_pool + j, cur_rows, :]
        y_act.append(_gelu_tanh(y).astype(BF16))

    lane_t = lax.broadcasted_iota(jnp.int32, (TT, LANES), 1)
    heads_per_vreg = LANES // hd

    def attend(b):
        rows = slice(b * TT, (b + 1) * TT)
        es, inv = [], []
        for hh in range(MEM_HEADS):
            sh = s_ref[rows, hh * n_mem:(hh + 1) * n_mem]
            e = jnp.exp2(sh - jnp.max(sh, axis=-1, keepdims=True))
            inv.append(1.0 / jnp.sum(e, axis=-1, keepdims=True))
            es.append(e.astype(BF16))
        o = jnp.dot(jnp.concatenate(es, axis=1), vbd_s[b], preferred_element_type=F32)
        norm = []
        for c in range(att_w // LANES):
            sc = inv[c * heads_per_vreg]
            for e_i in range(1, heads_per_vreg):
                sc = jnp.where(lane_t < e_i * hd, sc, inv[c * heads_per_vreg + e_i])
            norm.append(sc)
        o = o * jnp.concatenate(norm, axis=1)
        ycat_ref[rows, mix_w:] = (o * gate_ref[rows, mix_w:].astype(F32)).astype(BF16)

    for b in range(NB // 2):
        attend(b)

    z = jnp.dot(jnp.concatenate(y_act, axis=1), wglu_ref[...], preferred_element_type=F32)
    y_ssm = z[:, 0:ssm_w] * _sigmoid(z[:, ssm_w:])
    for j in range(n_ssm):
        utb_ref[n_pool + j, cur_rows, :] = y_ssm[:, j * LANES:(j + 1) * LANES]

    for b in range(NB // 2, NB):
        attend(b)

    for c in range(NB // BC):
        crow = slice(c * BC * TT, (c + 1) * BC * TT)
        for bb in range(BC):
            b = c * BC + bb
            rows = slice(b * TT, (b + 1) * TT)
            for k in range(n_slab):
                cols = slice(k * LANES, (k + 1) * LANES)
                yk = utb_ref[k, pl.ds(HALO * NB + b, TT, stride=NB), :]
                ycat_ref[rows, cols] = (yk * gate_ref[rows, cols].astype(F32)).astype(BF16)
        out = jnp.dot(ycat_ref[crow, :], wout_s[...], preferred_element_type=F32)
        ms2 = jnp.mean(out * out, axis=-1, keepdims=True)
        res = x_ref[c * BC:(c + 1) * BC].reshape(BC * TT, d) + (out * lax.rsqrt(ms2 + EPS)) * gpost_ref[...]
        out_ref[c * BC:(c + 1) * BC] = res.reshape(BC, TT, d)


def _const_spec(shape):
    zeros = (0,) * len(shape)
    return pl.BlockSpec(shape, lambda bi, ti: zeros, pipeline_mode=pl.Buffered(1))


def _layer_call(x, kt, v, g_pre, w_in, wpool_bd, pscale, inv_w, inv_c0, wtb, wc, a_mat, d_skip, w_glu, w_out, g_post,
                *, pool_w, ssm_w, att_w, n_mem):
    bsz, seq, d = x.shape
    assert bsz % NB == 0 and seq % TT == 0 and TT % TB == 0 and TT % TC == 0 and TB == HALO and NB % BC == 0
    assert pool_w % LANES == 0 and ssm_w % LANES == 0 and att_w % LANES == 0 and att_w == n_mem
    n_pool, n_ssm = pool_w // LANES, ssm_w // LANES
    n_slab = n_pool + n_ssm
    n_state = a_mat.shape[-1]
    m_rows = NB * TT
    kern = functools.partial(_layer_kernel, d=d, pool_w=pool_w, ssm_w=ssm_w, att_w=att_w, n_mem=n_mem)
    consts = (g_pre, w_in, wpool_bd, pscale, inv_w, inv_c0, wtb, wc, a_mat, d_skip, w_glu, w_out, g_post)
    return pl.pallas_call(
        kern,
        grid=(bsz // NB, seq // TT),
        in_specs=[
            pl.BlockSpec((NB, TT, d), lambda bi, ti: (bi, ti, 0)),
            pl.BlockSpec(memory_space=pl.ANY),
            pl.BlockSpec(memory_space=pl.ANY),
        ] + [pl.BlockSpec(memory_space=pl.ANY) if c is w_in or c is w_out else _const_spec(c.shape) for c in consts],
        out_specs=pl.BlockSpec((NB, TT, d), lambda bi, ti: (bi, ti, 0)),
        out_shape=jax.ShapeDtypeStruct(x.shape, x.dtype),
        scratch_shapes=[
            pltpu.VMEM((n_slab, (HALO + TT) * NB, LANES), F32),
            pltpu.VMEM((m_rows, d), BF16),
            pltpu.VMEM((m_rows, att_w), BF16),
            pltpu.VMEM((m_rows, d), BF16),
            pltpu.VMEM((m_rows, pool_w), BF16),
            pltpu.VMEM((n_ssm, m_rows // TC, n_state), F32),
            pltpu.VMEM((n_ssm, m_rows // TC, n_state), F32),
            pltpu.VMEM((n_ssm, NB, n_state), F32),
            pltpu.VMEM((m_rows, MEM_HEADS * n_mem), F32),
            pltpu.VMEM(w_in.shape, BF16),
            pltpu.VMEM(w_out.shape, BF16),
            pltpu.VMEM((NB, att_w, MEM_HEADS * n_mem), BF16),
            pltpu.VMEM((NB, MEM_HEADS * n_mem, att_w), BF16),
            pltpu.VMEM((NB, att_w, n_mem), BF16),
        ],
        compiler_params=pltpu.CompilerParams(
            dimension_semantics=("arbitrary", "arbitrary"),
            vmem_limit_bytes=VMEM_LIMIT_BYTES,
        ),
        name="layer_call",
    )(x, kt, v, *consts)


def _expand_kernel(xt_ref, xb_ref, xc_ref, wtb_ref, wc_ref, *, c, n_st):
    gpb = LANES // c
    tl = TC * LANES

    def rep(rows, cols, row_blk, col_blk, period):
        r = lax.broadcasted_iota(jnp.int32, (rows, cols), 0)
        q = lax.broadcasted_iota(jnp.int32, (rows, cols), 1)
        hit = (r // row_blk == q // col_blk) & (r % period == q % period)
        return jnp.where(hit, 1.0, 0.0).astype(BF16)

    def same_group(rows, cols, row_div, col_div):
        r = lax.broadcasted_iota(jnp.int32, (rows, cols), 0)
        q = lax.broadcasted_iota(jnp.int32, (rows, cols), 1)
        return (r // row_div) % gpb == (q // col_div) % gpb

    rep_t = rep(TC * c, tl, c, LANES, c)
    rep_b = rep(2 * n_st, 2 * gpb * n_st, n_st, gpb * n_st, n_st)
    wt = jnp.dot(xt_ref[0].astype(BF16), rep_t, preferred_element_type=F32)
    wtb_ref[0, :, 0:tl] = jnp.where(same_group(tl, tl, c, c), wt, 0.0).astype(BF16)
    wb = jnp.dot(xb_ref[0].astype(BF16), rep_b, preferred_element_type=F32)
    wtb_ref[0, :, tl:] = jnp.where(same_group(tl, 2 * gpb * n_st, c, n_st), wb, 0.0).astype(BF16)
    wcx = jnp.dot(xc_ref[0].astype(BF16), rep_t, preferred_element_type=F32)
    wc_ref[0] = jnp.where(same_group(2 * gpb * n_st, tl, n_st, c), wcx, 0.0).astype(BF16)


def _ssm_matrices(a_re, a_im, log_dt, b_re, b_im, c_re, c_im):
    n_grp, n_st = a_re.shape
    c = b_re.shape[-1]
    gpb = LANES // c
    nj = n_grp // gpb
    hi = lax.Precision.HIGHEST
    dt = jnp.exp(log_dt)[:, None]
    ks = jnp.arange(TC + 1, dtype=F32)[:, None, None]
    mag = jnp.exp(ks * (a_re * dt)[None])
    ang = ks * (a_im * dt)[None]
    p_re, p_im = mag * jnp.cos(ang), mag * jnp.sin(ang)
    den = a_re * a_re + a_im * a_im
    x_re, x_im = p_re[1] - 1.0, p_im[1]
    f_re = (x_re * a_re + x_im * a_im) / den
    f_im = (x_im * a_re - x_re * a_im) / den
    bt_re, bt_im = jnp.swapaxes(b_re, 1, 2), jnp.swapaxes(b_im, 1, 2)
    bb_re = f_re[:, None, :] * bt_re - f_im[:, None, :] * bt_im
    bb_im = f_re[:, None, :] * bt_im + f_im[:, None, :] * bt_re
    cp_re = c_re[None] * p_re[:, :, None, :] - c_im[None] * p_im[:, :, None, :]
    cp_im = c_re[None] * p_im[:, :, None, :] + c_im[None] * p_re[:, :, None, :]
    kern = (jnp.einsum('gin,kgon->kgio', bb_re, cp_re[:TC], precision=hi)
            - jnp.einsum('gin,kgon->kgio', bb_im, cp_im[:TC], precision=hi))
    zero = jnp.zeros_like(kern[0])
    xt = jnp.stack([jnp.concatenate([kern[t - s] if t >= s else zero for t in range(TC)], axis=-1)
                    for s in range(TC)], axis=0)
    xt = xt.reshape(TC, nj, gpb * c, TC * c).swapaxes(0, 1).reshape(nj, TC * LANES, TC * c)
    q_re, q_im = p_re[TC - 1::-1][:, :, None, :], p_im[TC - 1::-1][:, :, None, :]
    xb = jnp.concatenate([q_re * bb_re[None] - q_im * bb_im[None], q_re * bb_im[None] + q_im * bb_re[None]], axis=-1)
    xb = xb.reshape(TC, nj, gpb * c, 2 * n_st).swapaxes(0, 1).reshape(nj, TC * LANES, 2 * n_st)
    xc = jnp.stack([cp_re[1:], -cp_im[1:]], axis=0)
    xc = jnp.transpose(xc, (2, 0, 4, 1, 3)).reshape(nj, gpb, 2, n_st, TC * c)
    xc = xc.swapaxes(1, 2).reshape(nj, 2 * gpb * n_st, TC * c)
    a_mat = jnp.concatenate([p_re[TC].reshape(nj, gpb * n_st), p_im[TC].reshape(nj, gpb * n_st)], axis=-1)
    a_mat = jnp.broadcast_to(a_mat[:, None, :], (nj, NB, a_mat.shape[-1]))
    n_state = 2 * gpb * n_st
    wtb, wc = pl.pallas_call(
        functools.partial(_expand_kernel, c=c, n_st=n_st),
        grid=(nj,),
        in_specs=[pl.BlockSpec((1,) + xt.shape[1:], lambda j: (j, 0, 0)),
                  pl.BlockSpec((1,) + xb.shape[1:], lambda j: (j, 0, 0)),
                  pl.BlockSpec((1,) + xc.shape[1:], lambda j: (j, 0, 0))],
        out_specs=[pl.BlockSpec((1, TC * LANES, TC * LANES + n_state), lambda j: (j, 0, 0)),
                   pl.BlockSpec((1, n_state, TC * LANES), lambda j: (j, 0, 0))],
        out_shape=[jax.ShapeDtypeStruct((nj, TC * LANES, TC * LANES + n_state), BF16),
                   jax.ShapeDtypeStruct((nj, n_state, TC * LANES), BF16)],
        compiler_params=pltpu.CompilerParams(dimension_semantics=("arbitrary",)),
        name="ssm_expand_call",
    )(xt, xb, xc)
    return wtb, wc, a_mat


def _pool_constants(w_pool, pool_w):
    n_grp, gw, _ = w_pool.shape
    wbd = jnp.zeros((pool_w, pool_w), F32)
    for gi in range(n_grp):
        wbd = wbd.at[gi * gw:(gi + 1) * gw, gi * gw:(gi + 1) * gw].set(w_pool[gi])
    win = jnp.repeat(jnp.asarray(POOL_WINDOWS, F32), gw)
    inv_w = (1.0 / win)[None, :]
    pos = jnp.repeat(jnp.arange(1, HALO + 1, dtype=F32), NB)[:, None]
    inv_c0 = 1.0 / jnp.minimum(pos, win[None, :])
    return wbd.astype(BF16), inv_w, inv_c0


def _layer(x, mem, g_pre, w_in, w_pool, pool_scale, a_re, a_im, log_dt, b_re, b_im, c_re, c_im, d_skip, w_glu,
           g_mem, w_kv, w_out, g_post):
    d = x.shape[-1]
    n_mem = mem.shape[1]
    pool_w = w_pool.shape[0] * w_pool.shape[1]
    ssm_w = d_skip.shape[0]
    att_w = w_kv.shape[1] // 2
    assert a_re.shape[0] * SSM_GROUP == ssm_w and b_re.shape[-1] == SSM_GROUP
    assert pool_w + ssm_w + att_w == d and w_pool.shape[0] == len(POOL_WINDOWS)

    kt, v = _kv_call(mem, g_mem[None, :], w_kv[:, :att_w].T.astype(BF16), w_kv[:, att_w:].astype(BF16))
    wtb, wc, a_mat = _ssm_matrices(a_re, a_im, log_dt, b_re, b_im, c_re, c_im)
    wpool_bd, inv_w, inv_c0 = _pool_constants(w_pool, pool_w)
    return _layer_call(
        x, kt, v, g_pre[None, :], w_in.astype(BF16), wpool_bd, pool_scale[None, :], inv_w, inv_c0,
        wtb, wc, a_mat, d_skip[None, :], w_glu.astype(BF16), w_out.astype(BF16), g_post[None, :],
        pool_w=pool_w, ssm_w=ssm_w, att_w=att_w, n_mem=n_mem)


def kernel(x, mem, g_pre, w_in, w_pool, pool_scale, a_re, a_im, log_dt, b_re, b_im, c_re, c_im, d_skip, w_glu, g_mem, w_kv, w_out, g_post):
    for i in range(g_pre.shape[0]):
        x = _layer(x, mem, g_pre[i], w_in[i], w_pool[i], pool_scale[i], a_re[i], a_im[i], log_dt[i], b_re[i], b_im[i],
                   c_re[i], c_im[i], d_skip[i], w_glu[i], g_mem[i], w_kv[i], w_out[i], g_post[i])
    return x
```

```python
import functools
import math

import jax
import jax.numpy as jnp
from jax import lax
from jax.experimental import pallas as pl
from jax.experimental.pallas import tpu as pltpu

LANES = 128
NB = 8
TT = 128
TC = 4
BC = 4
TB = 16
KV_MB = 4
HALO = 16
POOL_WINDOWS = (2, 4, 8, 16)
SSM_GROUP = 16
MEM_HEADS = 4
EPS = 1e-6
LOG2_E = 1.4426950408889634
GELU_C = 0.7978845608028654
V7X_VMEM_BYTES = 64 * 1024 * 1024
VMEM_RESERVE = 4 * 1024 * 1024
VMEM_SPILL_ALLOWANCE = 8 * 1024 * 1024

F32 = jnp.float32
BF16 = jnp.bfloat16


def _sigmoid(v):
    return 0.5 * jnp.tanh(0.5 * v) + 0.5


def _silu(v):
    hv = 0.5 * v
    return hv + hv * jnp.tanh(hv)


def _gelu_tanh(v):
    hv = 0.5 * v
    return hv + hv * jnp.tanh(v * (GELU_C + (GELU_C * 0.044715) * (v * v)))


def _kv_kernel(mem_ref, gmem_ref, wkt_ref, wv_ref, kt_ref, v_ref, *, att_w, n_mem):
    mb, _, d = mem_ref.shape
    m = mem_ref[...].reshape(mb * n_mem, d)
    ms = jnp.mean(m * m, axis=-1, keepdims=True)
    mn = ((m * lax.rsqrt(ms + EPS)) * gmem_ref[...]).astype(BF16)
    kt = lax.dot_general(wkt_ref[...], mn, (((1,), (1,)), ((), ())), preferred_element_type=F32)
    v = jnp.dot(mn, wv_ref[...], preferred_element_type=F32)
    hd = att_w // MEM_HEADS
    kt = kt * (hd ** -0.5 * LOG2_E)
    for i in range(mb):
        kt_ref[i] = kt[:, i * n_mem:(i + 1) * n_mem].astype(BF16)
        v_ref[i] = v[i * n_mem:(i + 1) * n_mem, :].astype(BF16)


def _kv_call(mem, g_mem, wkt, wv):
    bsz, n_mem, d = mem.shape
    att_w = wv.shape[1]
    assert bsz % KV_MB == 0
    kern = functools.partial(_kv_kernel, att_w=att_w, n_mem=n_mem)
    return pl.pallas_call(
        kern,
        grid=(bsz // KV_MB,),
        in_specs=[
            pl.BlockSpec((KV_MB, n_mem, d), lambda b: (b, 0, 0)),
            pl.BlockSpec((1, d), lambda b: (0, 0)),
            pl.BlockSpec((att_w, d), lambda b: (0, 0)),
            pl.BlockSpec((d, att_w), lambda b: (0, 0)),
        ],
        out_specs=[
            pl.BlockSpec((KV_MB, att_w, n_mem), lambda b: (b, 0, 0)),
            pl.BlockSpec((KV_MB, n_mem, att_w), lambda b: (b, 0, 0)),
        ],
        out_shape=[
            jax.ShapeDtypeStruct((bsz, att_w, n_mem), BF16),
            jax.ShapeDtypeStruct((bsz, n_mem, att_w), BF16),
        ],
        compiler_params=pltpu.CompilerParams(dimension_semantics=("arbitrary",)),
        name="kv_call",
    )(mem, g_mem, wkt, wv)


def _layer_kernel(x_ref, kt_ref, v_ref, gpre_ref, win_ref, wpool_ref, pscale_ref, invw_ref, invc0_ref,
                  wtb_ref, wc_ref, a_ref, dskip_ref, wglu_ref, wout_ref, gpost_ref,
                  out_ref,
                  utb_ref, gate_ref, q_ref, ycat_ref, pd_ref, g_ref, hin_ref, hc_ref,
                  s_ref, win_s, wout_s, kbd_s, vbd_s, kv_stage,
                  *, d, pool_w, ssm_w, att_w, n_mem):
    bi = pl.program_id(0)
    ti = pl.program_id(1)
    n_pool = pool_w // LANES
    n_ssm = ssm_w // LANES
    n_slab = n_pool + n_ssm
    mix_w = pool_w + ssm_w
    n_chunk = TT // TC
    st = hc_ref.shape[-1] // 2
    hd = att_w // MEM_HEADS
    cur_rows = slice(HALO * NB, (HALO + TT) * NB)

    @pl.when((bi == 0) & (ti == 0))
    def _():
        pltpu.sync_copy(win_ref, win_s)
        pltpu.sync_copy(wout_ref, wout_s)

    @pl.when(ti == 0)
    def _():
        group = pl.ds(pl.multiple_of(bi * NB, NB), NB)
        row_head = lax.broadcasted_iota(jnp.int32, (att_w, n_mem), 0) // hd
        col_head = lax.broadcasted_iota(jnp.int32, (n_mem, att_w), 1) // hd
        pltpu.sync_copy(kt_ref.at[group], kv_stage)
        for b in range(NB):
            kt_b = kv_stage[b].astype(F32)
            for hh in range(MEM_HEADS):
                kbd_s[b, :, hh * n_mem:(hh + 1) * n_mem] = jnp.where(row_head == hh, kt_b, 0.0).astype(BF16)
        pltpu.sync_copy(v_ref.at[group], kv_stage)
        for b in range(NB):
            v_b = kv_stage[b].astype(F32)
            for hh in range(MEM_HEADS):
                vbd_s[b, hh * n_mem:(hh + 1) * n_mem, :] = jnp.where(col_head == hh, v_b, 0.0).astype(BF16)
        utb_ref[0:n_pool, 0:HALO * NB, :] = jnp.zeros((n_pool, HALO * NB, LANES), F32)
        hc_ref[...] = jnp.zeros(hc_ref.shape, F32)

    for c in range(NB // BC):
        crow = slice(c * BC * TT, (c + 1) * BC * TT)
        x = x_ref[c * BC:(c + 1) * BC].reshape(BC * TT, d)
        ms = jnp.mean(x * x, axis=-1, keepdims=True)
        h = ((x * lax.rsqrt(ms + EPS)) * gpre_ref[...]).astype(BF16)
        val = jnp.dot(h, win_s[:, 0:mix_w], preferred_element_type=F32)
        for bb in range(BC):
            for k in range(n_slab):
                utb_ref[k, pl.ds(HALO * NB + c * BC + bb, TT, stride=NB), :] = (
                    val[bb * TT:(bb + 1) * TT, k * LANES:(k + 1) * LANES])
        q_ref[crow, :] = jnp.dot(h, win_s[:, mix_w:mix_w + att_w], preferred_element_type=F32).astype(BF16)
        gate = jnp.dot(h, win_s[:, mix_w + att_w:], preferred_element_type=F32)
        gate_ref[crow, :] = _silu(gate).astype(BF16)

    y_intra = []
    for j in range(n_ssm):
        u = utb_ref[n_pool + j, cur_rows, :]
        u4 = u.reshape(n_chunk, TC, NB, LANES)
        up = jnp.concatenate([u4[:, s].reshape(n_chunk * NB, LANES) for s in range(TC)], axis=1).astype(BF16)
        yg = jnp.dot(up, wtb_ref[j], preferred_element_type=F32)
        y_intra.append(yg[:, 0:TC * LANES])
        g_ref[j] = yg[:, TC * LANES:]

    for b in range(NB):
        rows = slice(b * TT, (b + 1) * TT)
        s_ref[rows, :] = jnp.dot(q_ref[rows, :], kbd_s[b], preferred_element_type=F32)

    gw = pool_w // len(POOL_WINDOWS)
    lane = lax.broadcasted_iota(jnp.int32, (TB * NB, LANES), 1)
    for k in range(n_pool):
        w_lo = POOL_WINDOWS[(k * LANES) // gw]
        w_hi = POOL_WINDOWS[min(((k + 1) * LANES - 1) // gw, len(POOL_WINDOWS) - 1)]
        split = ((k * LANES) // gw + 1) * gw - k * LANES
        for i in range(TT // TB):
            base = (HALO + i * TB) * NB
            ext = utb_ref[k, base - (HALO - 1) * NB: base + TB * NB, :]
            sums = {1: ext}
            w = 1
            while w < w_hi:
                prev = sums[w]
                sums[2 * w] = prev[w * NB:] + prev[:-w * NB]
                w *= 2
            cur = ext[(HALO - 1) * NB:]
            lo = sums[w_lo][(HALO - w_lo) * NB:]
            hi = sums[w_hi][(HALO - w_hi) * NB:]
            ssum = jnp.where(lane < split, lo, hi)
            scale = invw_ref[:, k * LANES:(k + 1) * LANES]
            if i == 0:
                scale = jnp.where(ti == 0, invc0_ref[:, k * LANES:(k + 1) * LANES], scale)
            pd_ref[i * TB * NB:(i + 1) * TB * NB, k * LANES:(k + 1) * LANES] = (ssum * scale - cur).astype(BF16)
    for k in range(n_pool):
        utb_ref[k, 0:HALO * NB, :] = utb_ref[k, TT * NB:(TT + HALO) * NB, :]
    ypool = jnp.dot(pd_ref[...], wpool_ref[...], preferred_element_type=F32) * pscale_ref[...]
    for k in range(n_pool):
        utb_ref[k, cur_rows, :] = ypool[:, k * LANES:(k + 1) * LANES]

    a_re = [a_ref[j, :, 0:st] for j in range(n_ssm)]
    a_im = [a_ref[j, :, st:] for j in range(n_ssm)]
    h_re = [hc_ref[j, :, 0:st] for j in range(n_ssm)]
    h_im = [hc_ref[j, :, st:] for j in range(n_ssm)]
    for r in range(n_chunk):
        rows = slice(r * NB, (r + 1) * NB)
        for j in range(n_ssm):
            hin_ref[j, rows, 0:st] = h_re[j]
            hin_ref[j, rows, st:] = h_im[j]
            g_re = g_ref[j, rows, 0:st]
            g_im = g_ref[j, rows, st:]
            h_re[j], h_im[j] = (a_re[j] * h_re[j] - a_im[j] * h_im[j] + g_re,
                                a_re[j] * h_im[j] + a_im[j] * h_re[j] + g_im)
    for j in range(n_ssm):
        hc_ref[j, :, 0:st] = h_re[j]
        hc_ref[j, :, st:] = h_im[j]

    y_act = []
    for j in range(n_ssm):
        y_lin = y_intra[j] + jnp.dot(hin_ref[j].astype(BF16), wc_ref[j], preferred_element_type=F32)
        y_tb = jnp.stack([y_lin[:, s * LANES:(s + 1) * LANES].reshape(n_chunk, NB, LANES) for s in range(TC)],
                         axis=1).reshape(TT * NB, LANES)
        y = y_tb + dskip_ref[:, j * LANES:(j + 1) * LANES] * utb_ref[n_pool + j, cur_rows, :]
        y_act.append(_gelu_tanh(y).astype(BF16))

    lane_t = lax.broadcasted_iota(jnp.int32, (TT, LANES), 1)
    heads_per_vreg = LANES // hd

    def attend(b):
        rows = slice(b * TT, (b + 1) * TT)
        es, inv = [], []
        for hh in range(MEM_HEADS):
            sh = s_ref[rows, hh * n_mem:(hh + 1) * n_mem]
            e = jnp.exp2(sh - jnp.max(sh, axis=-1, keepdims=True))
            inv.append(1.0 / jnp.sum(e, axis=-1, keepdims=True))
            es.append(e.astype(BF16))
        o = jnp.dot(jnp.concatenate(es, axis=1), vbd_s[b], preferred_element_type=F32)
        norm = []
        for c in range(att_w // LANES):
            sc = inv[c * heads_per_vreg]
            for e_i in range(1, heads_per_vreg):
                sc = jnp.where(lane_t < e_i * hd, sc, inv[c * heads_per_vreg + e_i])
            norm.append(sc)
        o = o * jnp.concatenate(norm, axis=1)
        ycat_ref[rows, mix_w:] = (o * gate_ref[rows, mix_w:].astype(F32)).astype(BF16)

    for b in range(NB // 2):
        attend(b)

    z = jnp.dot(jnp.concatenate(y_act, axis=1), wglu_ref[...], preferred_element_type=F32)
    y_ssm = z[:, 0:ssm_w] * _sigmoid(z[:, ssm_w:])
    for j in range(n_ssm):
        utb_ref[n_pool + j, cur_rows, :] = y_ssm[:, j * LANES:(j + 1) * LANES]

    for b in range(NB // 2, NB):
        attend(b)

    for c in range(NB // BC):
        crow = slice(c * BC * TT, (c + 1) * BC * TT)
        for bb in range(BC):
            b = c * BC + bb
            rows = slice(b * TT, (b + 1) * TT)
            for k in range(n_slab):
                cols = slice(k * LANES, (k + 1) * LANES)
                yk = utb_ref[k, pl.ds(HALO * NB + b, TT, stride=NB), :]
                ycat_ref[rows, cols] = (yk * gate_ref[rows, cols].astype(F32)).astype(BF16)
        out = jnp.dot(ycat_ref[crow, :], wout_s[...], preferred_element_type=F32)
        ms2 = jnp.mean(out * out, axis=-1, keepdims=True)
        res = x_ref[c * BC:(c + 1) * BC].reshape(BC * TT, d) + (out * lax.rsqrt(ms2 + EPS)) * gpost_ref[...]
        out_ref[c * BC:(c + 1) * BC] = res.reshape(BC, TT, d)


def _const_spec(shape):
    zeros = (0,) * len(shape)
    return pl.BlockSpec(shape, lambda bi, ti: zeros, pipeline_mode=pl.Buffered(1))


def _layer_call(x, kt, v, g_pre, w_in, wpool_bd, pscale, inv_w, inv_c0, wtb, wc, a_mat, d_skip, w_glu, w_out, g_post,
                *, pool_w, ssm_w, att_w, n_mem):
    bsz, seq, d = x.shape
    assert bsz % NB == 0 and seq % TT == 0 and TT % TB == 0 and TT % TC == 0 and TB == HALO and NB % BC == 0
    assert pool_w % LANES == 0 and ssm_w % LANES == 0 and att_w % LANES == 0 and att_w == n_mem
    n_pool, n_ssm = pool_w // LANES, ssm_w // LANES
    n_slab = n_pool + n_ssm
    n_state = a_mat.shape[-1]
    m_rows = NB * TT
    kern = functools.partial(_layer_kernel, d=d, pool_w=pool_w, ssm_w=ssm_w, att_w=att_w, n_mem=n_mem)
    consts = (g_pre, w_in, wpool_bd, pscale, inv_w, inv_c0, wtb, wc, a_mat, d_skip, w_glu, w_out, g_post)
    in_hbm = (w_in, w_out)
    scratch = [
        ((n_slab, (HALO + TT) * NB, LANES), F32),
        ((m_rows, d), BF16),
        ((m_rows, att_w), BF16),
        ((m_rows, d), BF16),
        ((m_rows, pool_w), BF16),
        ((n_ssm, m_rows // TC, n_state), F32),
        ((n_ssm, m_rows // TC, n_state), F32),
        ((n_ssm, NB, n_state), F32),
        ((m_rows, MEM_HEADS * n_mem), F32),
        (w_in.shape, BF16),
        (w_out.shape, BF16),
        ((NB, att_w, MEM_HEADS * n_mem), BF16),
        ((NB, MEM_HEADS * n_mem, att_w), BF16),
        ((NB, att_w, n_mem), BF16),
    ]
    nbytes = lambda shape, dt: math.prod(shape) * jnp.dtype(dt).itemsize
    declared = (sum(nbytes(s, dt) for s, dt in scratch) + 2 * 2 * nbytes((NB, TT, d), x.dtype)
                + sum(nbytes(c.shape, c.dtype) for c in consts if not any(c is h for h in in_hbm)))
    vmem_limit = min(declared + VMEM_SPILL_ALLOWANCE, V7X_VMEM_BYTES - VMEM_RESERVE)
    return pl.pallas_call(
        kern,
        grid=(bsz // NB, seq // TT),
        in_specs=[
            pl.BlockSpec((NB, TT, d), lambda bi, ti: (bi, ti, 0)),
            pl.BlockSpec(memory_space=pl.ANY),
            pl.BlockSpec(memory_space=pl.ANY),
        ] + [pl.BlockSpec(memory_space=pl.ANY) if any(c is h for h in in_hbm) else _const_spec(c.shape)
             for c in consts],
        out_specs=pl.BlockSpec((NB, TT, d), lambda bi, ti: (bi, ti, 0)),
        out_shape=jax.ShapeDtypeStruct(x.shape, x.dtype),
        scratch_shapes=[pltpu.VMEM(s, dt) for s, dt in scratch],
        compiler_params=pltpu.CompilerParams(
            dimension_semantics=("arbitrary", "arbitrary"),
            vmem_limit_bytes=vmem_limit,
        ),
        name="layer_call",
    )(x, kt, v, *consts)


def _expand_kernel(xt_ref, xb_ref, xc_ref, wtb_ref, wc_ref, *, c, n_st):
    gpb = LANES // c
    tl = TC * LANES

    def rep(rows, cols, row_blk, col_blk, period):
        r = lax.broadcasted_iota(jnp.int32, (rows, cols), 0)
        q = lax.broadcasted_iota(jnp.int32, (rows, cols), 1)
        hit = (r // row_blk == q // col_blk) & (r % period == q % period)
        return jnp.where(hit, 1.0, 0.0).astype(BF16)

    def same_group(rows, cols, row_div, col_div):
        r = lax.broadcasted_iota(jnp.int32, (rows, cols), 0)
        q = lax.broadcasted_iota(jnp.int32, (rows, cols), 1)
        return (r // row_div) % gpb == (q // col_div) % gpb

    rep_t = rep(TC * c, tl, c, LANES, c)
    rep_b = rep(2 * n_st, 2 * gpb * n_st, n_st, gpb * n_st, n_st)
    wt = jnp.dot(xt_ref[0].astype(BF16), rep_t, preferred_element_type=F32)
    wtb_ref[0, :, 0:tl] = jnp.where(same_group(tl, tl, c, c), wt, 0.0).astype(BF16)
    wb = jnp.dot(xb_ref[0].astype(BF16), rep_b, preferred_element_type=F32)
    wtb_ref[0, :, tl:] = jnp.where(same_group(tl, 2 * gpb * n_st, c, n_st), wb, 0.0).astype(BF16)
    wcx = jnp.dot(xc_ref[0].astype(BF16), rep_t, preferred_element_type=F32)
    wc_ref[0] = jnp.where(same_group(2 * gpb * n_st, tl, n_st, c), wcx, 0.0).astype(BF16)


def _ssm_matrices(a_re, a_im, log_dt, b_re, b_im, c_re, c_im):
    n_grp, n_st = a_re.shape
    c = b_re.shape[-1]
    gpb = LANES // c
    nj = n_grp // gpb
    dt = jnp.exp(log_dt)[:, None]
    ks = jnp.arange(TC + 1, dtype=F32)[:, None, None]
    mag = jnp.exp(ks * (a_re * dt)[None])
    ang = ks * (a_im * dt)[None]
    p_re, p_im = mag * jnp.cos(ang), mag * jnp.sin(ang)
    den = a_re * a_re + a_im * a_im
    x_re, x_im = p_re[1] - 1.0, p_im[1]
    f_re = (x_re * a_re + x_im * a_im) / den
    f_im = (x_im * a_re - x_re * a_im) / den
    bt_re, bt_im = jnp.swapaxes(b_re, 1, 2), jnp.swapaxes(b_im, 1, 2)
    bb_re = f_re[:, None, :] * bt_re - f_im[:, None, :] * bt_im
    bb_im = f_re[:, None, :] * bt_im + f_im[:, None, :] * bt_re
    cp_re = c_re[None] * p_re[:, :, None, :] - c_im[None] * p_im[:, :, None, :]
    cp_im = c_re[None] * p_im[:, :, None, :] + c_im[None] * p_re[:, :, None, :]
    kern = jnp.sum(bb_re[None, :, :, None, :] * cp_re[:TC, :, None, :, :]
                   - bb_im[None, :, :, None, :] * cp_im[:TC, :, None, :, :], axis=-1)
    zero = jnp.zeros_like(kern[0])
    xt = jnp.stack([jnp.concatenate([kern[t - s] if t >= s else zero for t in range(TC)], axis=-1)
                    for s in range(TC)], axis=0)
    xt = xt.reshape(TC, nj, gpb * c, TC * c).swapaxes(0, 1).reshape(nj, TC * LANES, TC * c)
    q_re, q_im = p_re[TC - 1::-1][:, :, None, :], p_im[TC - 1::-1][:, :, None, :]
    xb = jnp.concatenate([q_re * bb_re[None] - q_im * bb_im[None], q_re * bb_im[None] + q_im * bb_re[None]], axis=-1)
    xb = xb.reshape(TC, nj, gpb * c, 2 * n_st).swapaxes(0, 1).reshape(nj, TC * LANES, 2 * n_st)
    xc = jnp.stack([cp_re[1:], -cp_im[1:]], axis=0)
    xc = jnp.transpose(xc, (2, 0, 4, 1, 3)).reshape(nj, gpb, 2, n_st, TC * c)
    xc = xc.swapaxes(1, 2).reshape(nj, 2 * gpb * n_st, TC * c)
    a_mat = jnp.concatenate([p_re[TC].reshape(nj, gpb * n_st), p_im[TC].reshape(nj, gpb * n_st)], axis=-1)
    a_mat = jnp.broadcast_to(a_mat[:, None, :], (nj, NB, a_mat.shape[-1]))
    n_state = 2 * gpb * n_st
    wtb, wc = pl.pallas_call(
        functools.partial(_expand_kernel, c=c, n_st=n_st),
        grid=(nj,),
        in_specs=[pl.BlockSpec((1,) + xt.shape[1:], lambda j: (j, 0, 0)),
                  pl.BlockSpec((1,) + xb.shape[1:], lambda j: (j, 0, 0)),
                  pl.BlockSpec((1,) + xc.shape[1:], lambda j: (j, 0, 0))],
        out_specs=[pl.BlockSpec((1, TC * LANES, TC * LANES + n_state), lambda j: (j, 0, 0)),
                   pl.BlockSpec((1, n_state, TC * LANES), lambda j: (j, 0, 0))],
        out_shape=[jax.ShapeDtypeStruct((nj, TC * LANES, TC * LANES + n_state), BF16),
                   jax.ShapeDtypeStruct((nj, n_state, TC * LANES), BF16)],
        compiler_params=pltpu.CompilerParams(dimension_semantics=("arbitrary",)),
        name="ssm_expand_call",
    )(xt, xb, xc)
    return wtb, wc, a_mat


def _pool_constants(w_pool, pool_w):
    n_grp, gw, _ = w_pool.shape
    wbd = jnp.zeros((pool_w, pool_w), F32)
    for gi in range(n_grp):
        wbd = wbd.at[gi * gw:(gi + 1) * gw, gi * gw:(gi + 1) * gw].set(w_pool[gi])
    win = jnp.repeat(jnp.asarray(POOL_WINDOWS, F32), gw)
    inv_w = (1.0 / win)[None, :]
    pos = jnp.repeat(jnp.arange(1, HALO + 1, dtype=F32), NB)[:, None]
    inv_c0 = 1.0 / jnp.minimum(pos, win[None, :])
    return wbd.astype(BF16), inv_w, inv_c0


def _layer(x, mem, g_pre, w_in, w_pool, pool_scale, a_re, a_im, log_dt, b_re, b_im, c_re, c_im, d_skip, w_glu,
           g_mem, w_kv, w_out, g_post):
    d = x.shape[-1]
    n_mem = mem.shape[1]
    pool_w = w_pool.shape[0] * w_pool.shape[1]
    ssm_w = d_skip.shape[0]
    att_w = w_kv.shape[1] // 2
    assert a_re.shape[0] * SSM_GROUP == ssm_w and b_re.shape[-1] == SSM_GROUP
    assert pool_w + ssm_w + att_w == d and w_pool.shape[0] == len(POOL_WINDOWS)

    kt, v = _kv_call(mem, g_mem[None, :], w_kv[:, :att_w].T.astype(BF16), w_kv[:, att_w:].astype(BF16))
    wtb, wc, a_mat = _ssm_matrices(a_re, a_im, log_dt, b_re, b_im, c_re, c_im)
    wpool_bd, inv_w, inv_c0 = _pool_constants(w_pool, pool_w)
    return _layer_call(
        x, kt, v, g_pre[None, :], w_in.astype(BF16), wpool_bd, pool_scale[None, :], inv_w, inv_c0,
        wtb, wc, a_mat, d_skip[None, :], w_glu.astype(BF16), w_out.astype(BF16), g_post[None, :],
        pool_w=pool_w, ssm_w=ssm_w, att_w=att_w, n_mem=n_mem)


def kernel(x, mem, g_pre, w_in, w_pool, pool_scale, a_re, a_im, log_dt, b_re, b_im, c_re, c_im, d_skip, w_glu, g_mem, w_kv, w_out, g_post):
    for i in range(g_pre.shape[0]):
        x = _layer(x, mem, g_pre[i], w_in[i], w_pool[i], pool_scale[i], a_re[i], a_im[i], log_dt[i], b_re[i], b_im[i],
                   c_re[i], c_im[i], d_skip[i], w_glu[i], g_mem[i], w_kv[i], w_out[i], g_post[i])
    return x
```

```python
import functools
import math

import jax
import jax.numpy as jnp
from jax import lax
from jax.experimental import pallas as pl
from jax.experimental.pallas import tpu as pltpu

LANES = 128
NB = 8
TT = 128
TC = 4
BC = 4
TB = 16
KV_MB = 8
HALO = 16
POOL_WINDOWS = (2, 4, 8, 16)
SSM_GROUP = 16
MEM_HEADS = 4
EPS = 1e-6
LOG2_E = 1.4426950408889634
GELU_C = 0.7978845608028654
V7X_VMEM_BYTES = 64 * 1024 * 1024
VMEM_RESERVE = 4 * 1024 * 1024
VMEM_SPILL_ALLOWANCE = 8 * 1024 * 1024

F32 = jnp.float32
BF16 = jnp.bfloat16


def _sigmoid(v):
    return 0.5 * jnp.tanh(0.5 * v) + 0.5


def _silu(v):
    hv = 0.5 * v
    return hv + hv * jnp.tanh(hv)


def _gelu_tanh(v):
    hv = 0.5 * v
    return hv + hv * jnp.tanh(v * (GELU_C + (GELU_C * 0.044715) * (v * v)))


def _kv_kernel(mem_ref, gmem_ref, wkt_ref, wv_ref, kt_ref, v_ref, *, att_w, n_mem):
    mb, _, d = mem_ref.shape
    m = mem_ref[...].reshape(mb * n_mem, d)
    ms = jnp.mean(m * m, axis=-1, keepdims=True)
    mn = ((m * lax.rsqrt(ms + EPS)) * gmem_ref[...]).astype(BF16)
    kt = lax.dot_general(wkt_ref[...], mn, (((1,), (1,)), ((), ())), preferred_element_type=F32)
    v = jnp.dot(mn, wv_ref[...], preferred_element_type=F32)
    hd = att_w // MEM_HEADS
    kt = kt * (hd ** -0.5 * LOG2_E)
    for i in range(mb):
        kt_ref[i] = kt[:, i * n_mem:(i + 1) * n_mem].astype(BF16)
        v_ref[i] = v[i * n_mem:(i + 1) * n_mem, :].astype(BF16)


def _kv_call(mem, g_mem, wkt, wv):
    bsz, n_mem, d = mem.shape
    att_w = wv.shape[1]
    assert bsz % KV_MB == 0
    kern = functools.partial(_kv_kernel, att_w=att_w, n_mem=n_mem)
    return pl.pallas_call(
        kern,
        grid=(bsz // KV_MB,),
        in_specs=[
            pl.BlockSpec((KV_MB, n_mem, d), lambda b: (b, 0, 0)),
            pl.BlockSpec((1, d), lambda b: (0, 0)),
            pl.BlockSpec((att_w, d), lambda b: (0, 0)),
            pl.BlockSpec((d, att_w), lambda b: (0, 0)),
        ],
        out_specs=[
            pl.BlockSpec((KV_MB, att_w, n_mem), lambda b: (b, 0, 0)),
            pl.BlockSpec((KV_MB, n_mem, att_w), lambda b: (b, 0, 0)),
        ],
        out_shape=[
            jax.ShapeDtypeStruct((bsz, att_w, n_mem), BF16),
            jax.ShapeDtypeStruct((bsz, n_mem, att_w), BF16),
        ],
        compiler_params=pltpu.CompilerParams(dimension_semantics=("arbitrary",)),
        name="kv_call",
    )(mem, g_mem, wkt, wv)


def _layer_kernel(x_ref, kt_ref, v_ref, gpre_ref, win_ref, wpool_ref, pscale_ref, invw_ref, invc0_ref,
                  wtb_ref, wc_ref, a_ref, dskip_ref, wglu_ref, wout_ref, gpost_ref,
                  out_ref,
                  utb_ref, gate_ref, q_ref, ycat_ref, pd_ref, g_ref, hin_ref, hc_ref,
                  s_ref, win_s, wout_s, kbd_s, vbd_s, kv_stage,
                  *, d, pool_w, ssm_w, att_w, n_mem):
    bi = pl.program_id(0)
    ti = pl.program_id(1)
    n_pool = pool_w // LANES
    n_ssm = ssm_w // LANES
    n_slab = n_pool + n_ssm
    mix_w = pool_w + ssm_w
    n_chunk = TT // TC
    st = hc_ref.shape[-1] // 2
    hd = att_w // MEM_HEADS
    cur_rows = slice(HALO * NB, (HALO + TT) * NB)

    @pl.when((bi == 0) & (ti == 0))
    def _():
        pltpu.sync_copy(win_ref, win_s)
        pltpu.sync_copy(wout_ref, wout_s)

    @pl.when(ti == 0)
    def _():
        group = pl.ds(pl.multiple_of(bi * NB, NB), NB)
        row_head = lax.broadcasted_iota(jnp.int32, (att_w, n_mem), 0) // hd
        col_head = lax.broadcasted_iota(jnp.int32, (n_mem, att_w), 1) // hd
        pltpu.sync_copy(kt_ref.at[group], kv_stage)
        for b in range(NB):
            kt_b = kv_stage[b].astype(F32)
            for hh in range(MEM_HEADS):
                kbd_s[b, :, hh * n_mem:(hh + 1) * n_mem] = jnp.where(row_head == hh, kt_b, 0.0).astype(BF16)
        pltpu.sync_copy(v_ref.at[group], kv_stage)
        for b in range(NB):
            v_b = kv_stage[b].astype(F32)
            for hh in range(MEM_HEADS):
                vbd_s[b, hh * n_mem:(hh + 1) * n_mem, :] = jnp.where(col_head == hh, v_b, 0.0).astype(BF16)
        utb_ref[0:n_pool, 0:HALO * NB, :] = jnp.zeros((n_pool, HALO * NB, LANES), F32)
        hc_ref[...] = jnp.zeros(hc_ref.shape, F32)

    for c in range(NB // BC):
        crow = slice(c * BC * TT, (c + 1) * BC * TT)
        x = x_ref[c * BC:(c + 1) * BC].reshape(BC * TT, d)
        ms = jnp.mean(x * x, axis=-1, keepdims=True)
        h = ((x * lax.rsqrt(ms + EPS)) * gpre_ref[...]).astype(BF16)
        val = jnp.dot(h, win_s[:, 0:mix_w], preferred_element_type=F32)
        for bb in range(BC):
            for k in range(n_slab):
                utb_ref[k, pl.ds(HALO * NB + c * BC + bb, TT, stride=NB), :] = (
                    val[bb * TT:(bb + 1) * TT, k * LANES:(k + 1) * LANES])
        q_ref[crow, :] = jnp.dot(h, win_s[:, mix_w:mix_w + att_w], preferred_element_type=F32).astype(BF16)
        gate = jnp.dot(h, win_s[:, mix_w + att_w:], preferred_element_type=F32)
        gate_ref[crow, :] = _silu(gate).astype(BF16)

    y_intra = []
    for j in range(n_ssm):
        u = utb_ref[n_pool + j, cur_rows, :]
        u4 = u.reshape(n_chunk, TC, NB, LANES)
        up = jnp.concatenate([u4[:, s].reshape(n_chunk * NB, LANES) for s in range(TC)], axis=1).astype(BF16)
        yg = jnp.dot(up, wtb_ref[j], preferred_element_type=F32)
        y_intra.append(yg[:, 0:TC * LANES])
        g_ref[j] = yg[:, TC * LANES:]

    for b in range(NB):
        rows = slice(b * TT, (b + 1) * TT)
        s_ref[rows, :] = jnp.dot(q_ref[rows, :], kbd_s[b], preferred_element_type=F32)

    gw = pool_w // len(POOL_WINDOWS)
    lane = lax.broadcasted_iota(jnp.int32, (TB * NB, LANES), 1)
    for k in range(n_pool):
        w_lo = POOL_WINDOWS[(k * LANES) // gw]
        w_hi = POOL_WINDOWS[min(((k + 1) * LANES - 1) // gw, len(POOL_WINDOWS) - 1)]
        split = ((k * LANES) // gw + 1) * gw - k * LANES
        for i in range(TT // TB):
            base = (HALO + i * TB) * NB
            ext = utb_ref[k, base - (HALO - 1) * NB: base + TB * NB, :]
            sums = {1: ext}
            w = 1
            while w < w_hi:
                prev = sums[w]
                sums[2 * w] = prev[w * NB:] + prev[:-w * NB]
                w *= 2
            cur = ext[(HALO - 1) * NB:]
            lo = sums[w_lo][(HALO - w_lo) * NB:]
            hi = sums[w_hi][(HALO - w_hi) * NB:]
            ssum = jnp.where(lane < split, lo, hi)
            scale = invw_ref[:, k * LANES:(k + 1) * LANES]
            if i == 0:
                scale = jnp.where(ti == 0, invc0_ref[:, k * LANES:(k + 1) * LANES], scale)
            pd_ref[i * TB * NB:(i + 1) * TB * NB, k * LANES:(k + 1) * LANES] = (ssum * scale - cur).astype(BF16)
    for k in range(n_pool):
        utb_ref[k, 0:HALO * NB, :] = utb_ref[k, TT * NB:(TT + HALO) * NB, :]
    ypool = jnp.dot(pd_ref[...], wpool_ref[...], preferred_element_type=F32) * pscale_ref[...]
    for k in range(n_pool):
        utb_ref[k, cur_rows, :] = ypool[:, k * LANES:(k + 1) * LANES]

    a_re = [a_ref[j, :, 0:st] for j in range(n_ssm)]
    a_im = [a_ref[j, :, st:] for j in range(n_ssm)]
    h_re = [hc_ref[j, :, 0:st] for j in range(n_ssm)]
    h_im = [hc_ref[j, :, st:] for j in range(n_ssm)]
    for r in range(n_chunk):
        rows = slice(r * NB, (r + 1) * NB)
        for j in range(n_ssm):
            hin_ref[j, rows, 0:st] = h_re[j]
            hin_ref[j, rows, st:] = h_im[j]
            g_re = g_ref[j, rows, 0:st]
            g_im = g_ref[j, rows, st:]
            h_re[j], h_im[j] = (a_re[j] * h_re[j] - a_im[j] * h_im[j] + g_re,
                                a_re[j] * h_im[j] + a_im[j] * h_re[j] + g_im)
    for j in range(n_ssm):
        hc_ref[j, :, 0:st] = h_re[j]
        hc_ref[j, :, st:] = h_im[j]

    y_act = []
    for j in range(n_ssm):
        y_lin = y_intra[j] + jnp.dot(hin_ref[j].astype(BF16), wc_ref[j], preferred_element_type=F32)
        y_tb = jnp.stack([y_lin[:, s * LANES:(s + 1) * LANES].reshape(n_chunk, NB, LANES) for s in range(TC)],
                         axis=1).reshape(TT * NB, LANES)
        y = y_tb + dskip_ref[:, j * LANES:(j + 1) * LANES] * utb_ref[n_pool + j, cur_rows, :]
        y_act.append(_gelu_tanh(y).astype(BF16))

    lane_t = lax.broadcasted_iota(jnp.int32, (TT, LANES), 1)
    heads_per_vreg = LANES // hd

    def attend(b):
        rows = slice(b * TT, (b + 1) * TT)
        es, inv = [], []
        for hh in range(MEM_HEADS):
            sh = s_ref[rows, hh * n_mem:(hh + 1) * n_mem]
            e = jnp.exp2(sh - jnp.max(sh, axis=-1, keepdims=True))
            inv.append(1.0 / jnp.sum(e, axis=-1, keepdims=True))
            es.append(e.astype(BF16))
        o = jnp.dot(jnp.concatenate(es, axis=1), vbd_s[b], preferred_element_type=F32)
        norm = []
        for c in range(att_w // LANES):
            sc = inv[c * heads_per_vreg]
            for e_i in range(1, heads_per_vreg):
                sc = jnp.where(lane_t < e_i * hd, sc, inv[c * heads_per_vreg + e_i])
            norm.append(sc)
        o = o * jnp.concatenate(norm, axis=1)
        ycat_ref[rows, mix_w:] = (o * gate_ref[rows, mix_w:].astype(F32)).astype(BF16)

    for b in range(NB // 2):
        attend(b)

    z = jnp.dot(jnp.concatenate(y_act, axis=1), wglu_ref[...], preferred_element_type=F32)
    y_ssm = z[:, 0:ssm_w] * _sigmoid(z[:, ssm_w:])
    for j in range(n_ssm):
        utb_ref[n_pool + j, cur_rows, :] = y_ssm[:, j * LANES:(j + 1) * LANES]

    for b in range(NB // 2, NB):
        attend(b)

    for c in range(NB // BC):
        crow = slice(c * BC * TT, (c + 1) * BC * TT)
        for bb in range(BC):
            b = c * BC + bb
            rows = slice(b * TT, (b + 1) * TT)
            for k in range(n_slab):
                cols = slice(k * LANES, (k + 1) * LANES)
                yk = utb_ref[k, pl.ds(HALO * NB + b, TT, stride=NB), :]
                ycat_ref[rows, cols] = (yk * gate_ref[rows, cols].astype(F32)).astype(BF16)
        out = jnp.dot(ycat_ref[crow, :], wout_s[...], preferred_element_type=F32)
        ms2 = jnp.mean(out * out, axis=-1, keepdims=True)
        res = x_ref[c * BC:(c + 1) * BC].reshape(BC * TT, d) + (out * lax.rsqrt(ms2 + EPS)) * gpost_ref[...]
        out_ref[c * BC:(c + 1) * BC] = res.reshape(BC, TT, d)


def _const_spec(shape):
    zeros = (0,) * len(shape)
    return pl.BlockSpec(shape, lambda bi, ti: zeros, pipeline_mode=pl.Buffered(1))


def _layer_call(x, kt, v, g_pre, w_in, wpool_bd, pscale, inv_w, inv_c0, wtb, wc, a_mat, d_skip, w_glu, w_out, g_post,
                *, pool_w, ssm_w, att_w, n_mem):
    bsz, seq, d = x.shape
    assert bsz % NB == 0 and seq % TT == 0 and TT % TB == 0 and TT % TC == 0 and TB == HALO and NB % BC == 0
    assert pool_w % LANES == 0 and ssm_w % LANES == 0 and att_w % LANES == 0 and att_w == n_mem
    n_pool, n_ssm = pool_w // LANES, ssm_w // LANES
    n_slab = n_pool + n_ssm
    n_state = a_mat.shape[-1]
    m_rows = NB * TT
    kern = functools.partial(_layer_kernel, d=d, pool_w=pool_w, ssm_w=ssm_w, att_w=att_w, n_mem=n_mem)
    consts = (g_pre, w_in, wpool_bd, pscale, inv_w, inv_c0, wtb, wc, a_mat, d_skip, w_glu, w_out, g_post)
    in_hbm = (w_in, w_out)
    scratch = [
        ((n_slab, (HALO + TT) * NB, LANES), F32),
        ((m_rows, d), BF16),
        ((m_rows, att_w), BF16),
        ((m_rows, d), BF16),
        ((m_rows, pool_w), BF16),
        ((n_ssm, m_rows // TC, n_state), F32),
        ((n_ssm, m_rows // TC, n_state), F32),
        ((n_ssm, NB, n_state), F32),
        ((m_rows, MEM_HEADS * n_mem), F32),
        (w_in.shape, BF16),
        (w_out.shape, BF16),
        ((NB, att_w, MEM_HEADS * n_mem), BF16),
        ((NB, MEM_HEADS * n_mem, att_w), BF16),
        ((NB, att_w, n_mem), BF16),
    ]
    nbytes = lambda shape, dt: math.prod(shape) * jnp.dtype(dt).itemsize
    declared = (sum(nbytes(s, dt) for s, dt in scratch) + 2 * 2 * nbytes((NB, TT, d), x.dtype)
                + sum(nbytes(c.shape, c.dtype) for c in consts if not any(c is h for h in in_hbm)))
    vmem_limit = min(declared + VMEM_SPILL_ALLOWANCE, V7X_VMEM_BYTES - VMEM_RESERVE)
    return pl.pallas_call(
        kern,
        grid=(bsz // NB, seq // TT),
        in_specs=[
            pl.BlockSpec((NB, TT, d), lambda bi, ti: (bi, ti, 0)),
            pl.BlockSpec(memory_space=pl.ANY),
            pl.BlockSpec(memory_space=pl.ANY),
        ] + [pl.BlockSpec(memory_space=pl.ANY) if any(c is h for h in in_hbm) else _const_spec(c.shape)
             for c in consts],
        out_specs=pl.BlockSpec((NB, TT, d), lambda bi, ti: (bi, ti, 0)),
        out_shape=jax.ShapeDtypeStruct(x.shape, x.dtype),
        scratch_shapes=[pltpu.VMEM(s, dt) for s, dt in scratch],
        compiler_params=pltpu.CompilerParams(
            dimension_semantics=("arbitrary", "arbitrary"),
            vmem_limit_bytes=vmem_limit,
        ),
        name="layer_call",
    )(x, kt, v, *consts)


def _expand_kernel(xt_ref, xb_ref, xc_ref, wtb_ref, wc_ref, *, c, n_st):
    gpb = LANES // c
    tl = TC * LANES

    def rep(rows, cols, row_blk, col_blk, period):
        r = lax.broadcasted_iota(jnp.int32, (rows, cols), 0)
        q = lax.broadcasted_iota(jnp.int32, (rows, cols), 1)
        hit = (r // row_blk == q // col_blk) & (r % period == q % period)
        return jnp.where(hit, 1.0, 0.0).astype(BF16)

    def same_group(rows, cols, row_div, col_div):
        r = lax.broadcasted_iota(jnp.int32, (rows, cols), 0)
        q = lax.broadcasted_iota(jnp.int32, (rows, cols), 1)
        return (r // row_div) % gpb == (q // col_div) % gpb

    rep_t = rep(TC * c, tl, c, LANES, c)
    rep_b = rep(2 * n_st, 2 * gpb * n_st, n_st, gpb * n_st, n_st)
    wt = jnp.dot(xt_ref[0].astype(BF16), rep_t, preferred_element_type=F32)
    wtb_ref[0, :, 0:tl] = jnp.where(same_group(tl, tl, c, c), wt, 0.0).astype(BF16)
    wb = jnp.dot(xb_ref[0].astype(BF16), rep_b, preferred_element_type=F32)
    wtb_ref[0, :, tl:] = jnp.where(same_group(tl, 2 * gpb * n_st, c, n_st), wb, 0.0).astype(BF16)
    wcx = jnp.dot(xc_ref[0].astype(BF16), rep_t, preferred_element_type=F32)
    wc_ref[0] = jnp.where(same_group(2 * gpb * n_st, tl, n_st, c), wcx, 0.0).astype(BF16)


def _ssm_matrices(a_re, a_im, log_dt, b_re, b_im, c_re, c_im):
    n_grp, n_st = a_re.shape
    c = b_re.shape[-1]
    gpb = LANES // c
    nj = n_grp // gpb
    dt = jnp.exp(log_dt)[:, None]
    ks = jnp.arange(TC + 1, dtype=F32)[:, None, None]
    mag = jnp.exp(ks * (a_re * dt)[None])
    ang = ks * (a_im * dt)[None]
    p_re, p_im = mag * jnp.cos(ang), mag * jnp.sin(ang)
    den = a_re * a_re + a_im * a_im
    x_re, x_im = p_re[1] - 1.0, p_im[1]
    f_re = (x_re * a_re + x_im * a_im) / den
    f_im = (x_im * a_re - x_re * a_im) / den
    bt_re, bt_im = jnp.swapaxes(b_re, 1, 2), jnp.swapaxes(b_im, 1, 2)
    bb_re = f_re[:, None, :] * bt_re - f_im[:, None, :] * bt_im
    bb_im = f_re[:, None, :] * bt_im + f_im[:, None, :] * bt_re
    cp_re = c_re[None] * p_re[:, :, None, :] - c_im[None] * p_im[:, :, None, :]
    cp_im = c_re[None] * p_im[:, :, None, :] + c_im[None] * p_re[:, :, None, :]
    kern = jnp.sum(bb_re[None, :, :, None, :] * cp_re[:TC, :, None, :, :]
                   - bb_im[None, :, :, None, :] * cp_im[:TC, :, None, :, :], axis=-1)
    zero = jnp.zeros_like(kern[0])
    xt = jnp.stack([jnp.concatenate([kern[t - s] if t >= s else zero for t in range(TC)], axis=-1)
                    for s in range(TC)], axis=0)
    xt = xt.reshape(TC, nj, gpb * c, TC * c).swapaxes(0, 1).reshape(nj, TC * LANES, TC * c)
    q_re, q_im = p_re[TC - 1::-1][:, :, None, :], p_im[TC - 1::-1][:, :, None, :]
    xb = jnp.concatenate([q_re * bb_re[None] - q_im * bb_im[None], q_re * bb_im[None] + q_im * bb_re[None]], axis=-1)
    xb = xb.reshape(TC, nj, gpb * c, 2 * n_st).swapaxes(0, 1).reshape(nj, TC * LANES, 2 * n_st)
    xc = jnp.stack([cp_re[1:], -cp_im[1:]], axis=0)
    xc = jnp.transpose(xc, (2, 0, 4, 1, 3)).reshape(nj, gpb, 2, n_st, TC * c)
    xc = xc.swapaxes(1, 2).reshape(nj, 2 * gpb * n_st, TC * c)
    a_mat = jnp.concatenate([p_re[TC].reshape(nj, gpb * n_st), p_im[TC].reshape(nj, gpb * n_st)], axis=-1)
    a_mat = jnp.broadcast_to(a_mat[:, None, :], (nj, NB, a_mat.shape[-1]))
    n_state = 2 * gpb * n_st
    wtb, wc = pl.pallas_call(
        functools.partial(_expand_kernel, c=c, n_st=n_st),
        grid=(nj,),
        in_specs=[pl.BlockSpec((1,) + xt.shape[1:], lambda j: (j, 0, 0)),
                  pl.BlockSpec((1,) + xb.shape[1:], lambda j: (j, 0, 0)),
                  pl.BlockSpec((1,) + xc.shape[1:], lambda j: (j, 0, 0))],
        out_specs=[pl.BlockSpec((1, TC * LANES, TC * LANES + n_state), lambda j: (j, 0, 0)),
                   pl.BlockSpec((1, n_state, TC * LANES), lambda j: (j, 0, 0))],
        out_shape=[jax.ShapeDtypeStruct((nj, TC * LANES, TC * LANES + n_state), BF16),
                   jax.ShapeDtypeStruct((nj, n_state, TC * LANES), BF16)],
        compiler_params=pltpu.CompilerParams(dimension_semantics=("arbitrary",)),
        name="ssm_expand_call",
    )(xt, xb, xc)
    return wtb, wc, a_mat


def _pool_constants(w_pool, pool_w):
    n_grp, gw, _ = w_pool.shape
    wbd = (w_pool[:, :, None, :] * jnp.eye(n_grp, dtype=F32)[:, None, :, None]).reshape(pool_w, pool_w)
    win = jnp.repeat(jnp.asarray(POOL_WINDOWS, F32), gw)
    inv_w = (1.0 / win)[None, :]
    pos = jnp.repeat(jnp.arange(1, HALO + 1, dtype=F32), NB)[:, None]
    inv_c0 = 1.0 / jnp.minimum(pos, win[None, :])
    return wbd.astype(BF16), inv_w, inv_c0


def _layer(x, mem, g_pre, w_in, w_pool, pool_scale, a_re, a_im, log_dt, b_re, b_im, c_re, c_im, d_skip, w_glu,
           g_mem, w_kv, w_out, g_post):
    d = x.shape[-1]
    n_mem = mem.shape[1]
    pool_w = w_pool.shape[0] * w_pool.shape[1]
    ssm_w = d_skip.shape[0]
    att_w = w_kv.shape[1] // 2
    assert a_re.shape[0] * SSM_GROUP == ssm_w and b_re.shape[-1] == SSM_GROUP
    assert pool_w + ssm_w + att_w == d and w_pool.shape[0] == len(POOL_WINDOWS)

    kt, v = _kv_call(mem, g_mem[None, :], w_kv[:, :att_w].T.astype(BF16), w_kv[:, att_w:].astype(BF16))
    wtb, wc, a_mat = _ssm_matrices(a_re, a_im, log_dt, b_re, b_im, c_re, c_im)
    wpool_bd, inv_w, inv_c0 = _pool_constants(w_pool, pool_w)
    return _layer_call(
        x, kt, v, g_pre[None, :], w_in.astype(BF16), wpool_bd, pool_scale[None, :], inv_w, inv_c0,
        wtb, wc, a_mat, d_skip[None, :], w_glu.astype(BF16), w_out.astype(BF16), g_post[None, :],
        pool_w=pool_w, ssm_w=ssm_w, att_w=att_w, n_mem=n_mem)


def kernel(x, mem, g_pre, w_in, w_pool, pool_scale, a_re, a_im, log_dt, b_re, b_im, c_re, c_im, d_skip, w_glu, g_mem, w_kv, w_out, g_post):
    for i in range(g_pre.shape[0]):
        x = _layer(x, mem, g_pre[i], w_in[i], w_pool[i], pool_scale[i], a_re[i], a_im[i], log_dt[i], b_re[i], b_im[i],
                   c_re[i], c_im[i], d_skip[i], w_glu[i], g_mem[i], w_kv[i], w_out[i], g_post[i])
    return x
```

```python
import functools
import math

import jax
import jax.numpy as jnp
import numpy as np
from jax import lax
from jax.experimental import pallas as pl
from jax.experimental.pallas import tpu as pltpu

LANES = 128
NB = 8
TT = 128
TC = 4
BC = 4
TB = 16
KV_MB = 8
HALO = 16
POOL_WINDOWS = (2, 4, 8, 16)
SSM_GROUP = 16
MEM_HEADS = 4
EPS = 1e-6
LOG2_E = 1.4426950408889634
GELU_C = 0.7978845608028654
V7X_VMEM_BYTES = 64 * 1024 * 1024
VMEM_RESERVE = 4 * 1024 * 1024
VMEM_SPILL_ALLOWANCE = 8 * 1024 * 1024

F32 = jnp.float32
BF16 = jnp.bfloat16


def _sigmoid(v):
    return 0.5 * jnp.tanh(0.5 * v) + 0.5


def _silu(v):
    hv = 0.5 * v
    return hv + hv * jnp.tanh(hv)


def _gelu_tanh(v):
    hv = 0.5 * v
    return hv + hv * jnp.tanh(v * (GELU_C + (GELU_C * 0.044715) * (v * v)))


def _kv_kernel(mem_ref, gmem_ref, wk_ref, wv_ref, kt_ref, v_ref, *, att_w, n_mem):
    mb, _, d = mem_ref.shape
    m = mem_ref[...].reshape(mb * n_mem, d)
    ms = jnp.mean(m * m, axis=-1, keepdims=True)
    mn = ((m * lax.rsqrt(ms + EPS)) * gmem_ref[...]).astype(BF16)
    kt = lax.dot_general(wk_ref[...], mn, (((0,), (1,)), ((), ())), preferred_element_type=F32)
    v = jnp.dot(mn, wv_ref[...], preferred_element_type=F32)
    hd = att_w // MEM_HEADS
    kt = kt * (hd ** -0.5 * LOG2_E)
    for i in range(mb):
        kt_ref[i] = kt[:, i * n_mem:(i + 1) * n_mem].astype(BF16)
        v_ref[i] = v[i * n_mem:(i + 1) * n_mem, :].astype(BF16)


def _kv_call(mem, g_mem, wk, wv):
    bsz, n_mem, d = mem.shape
    att_w = wv.shape[1]
    assert bsz % KV_MB == 0
    kern = functools.partial(_kv_kernel, att_w=att_w, n_mem=n_mem)
    return pl.pallas_call(
        kern,
        grid=(bsz // KV_MB,),
        in_specs=[
            pl.BlockSpec((KV_MB, n_mem, d), lambda b: (b, 0, 0)),
            pl.BlockSpec((1, d), lambda b: (0, 0)),
            pl.BlockSpec((d, att_w), lambda b: (0, 0)),
            pl.BlockSpec((d, att_w), lambda b: (0, 0)),
        ],
        out_specs=[
            pl.BlockSpec((KV_MB, att_w, n_mem), lambda b: (b, 0, 0)),
            pl.BlockSpec((KV_MB, n_mem, att_w), lambda b: (b, 0, 0)),
        ],
        out_shape=[
            jax.ShapeDtypeStruct((bsz, att_w, n_mem), BF16),
            jax.ShapeDtypeStruct((bsz, n_mem, att_w), BF16),
        ],
        compiler_params=pltpu.CompilerParams(dimension_semantics=("arbitrary",)),
        name="kv_call",
    )(mem, g_mem, wk, wv)


def _layer_kernel(x_ref, kt_ref, v_ref, gpre_ref, win_ref, wpool_ref, pscale_ref, invw_ref, invc0_ref,
                  wtb_ref, wc_ref, a_ref, dskip_ref, wglu_ref, wout_ref, gpost_ref,
                  out_ref,
                  utb_ref, gate_ref, q_ref, ycat_ref, pd_ref, g_ref, hin_ref, hc_ref,
                  s_ref, win_s, wout_s, kbd_s, vbd_s, kv_stage,
                  *, d, pool_w, ssm_w, att_w, n_mem):
    bi = pl.program_id(0)
    ti = pl.program_id(1)
    n_pool = pool_w // LANES
    n_ssm = ssm_w // LANES
    n_slab = n_pool + n_ssm
    mix_w = pool_w + ssm_w
    n_chunk = TT // TC
    st = hc_ref.shape[-1] // 2
    hd = att_w // MEM_HEADS
    cur_rows = slice(HALO * NB, (HALO + TT) * NB)

    @pl.when((bi == 0) & (ti == 0))
    def _():
        pltpu.sync_copy(win_ref, win_s)
        pltpu.sync_copy(wout_ref, wout_s)

    @pl.when(ti == 0)
    def _():
        group = pl.ds(pl.multiple_of(bi * NB, NB), NB)
        row_head = lax.broadcasted_iota(jnp.int32, (att_w, n_mem), 0) // hd
        col_head = lax.broadcasted_iota(jnp.int32, (n_mem, att_w), 1) // hd
        pltpu.sync_copy(kt_ref.at[group], kv_stage)
        for b in range(NB):
            kt_b = kv_stage[b].astype(F32)
            for hh in range(MEM_HEADS):
                kbd_s[b, :, hh * n_mem:(hh + 1) * n_mem] = jnp.where(row_head == hh, kt_b, 0.0).astype(BF16)
        pltpu.sync_copy(v_ref.at[group], kv_stage)
        for b in range(NB):
            v_b = kv_stage[b].astype(F32)
            for hh in range(MEM_HEADS):
                vbd_s[b, hh * n_mem:(hh + 1) * n_mem, :] = jnp.where(col_head == hh, v_b, 0.0).astype(BF16)
        utb_ref[0:n_pool, 0:HALO * NB, :] = jnp.zeros((n_pool, HALO * NB, LANES), F32)
        hc_ref[...] = jnp.zeros(hc_ref.shape, F32)

    for c in range(NB // BC):
        crow = slice(c * BC * TT, (c + 1) * BC * TT)
        x = x_ref[c * BC:(c + 1) * BC].reshape(BC * TT, d)
        ms = jnp.mean(x * x, axis=-1, keepdims=True)
        h = ((x * lax.rsqrt(ms + EPS)) * gpre_ref[...]).astype(BF16)
        val = jnp.dot(h, win_s[:, 0:mix_w], preferred_element_type=F32)
        for bb in range(BC):
            for k in range(n_slab):
                utb_ref[k, pl.ds(HALO * NB + c * BC + bb, TT, stride=NB), :] = (
                    val[bb * TT:(bb + 1) * TT, k * LANES:(k + 1) * LANES])
        q_ref[crow, :] = jnp.dot(h, win_s[:, mix_w:mix_w + att_w], preferred_element_type=F32).astype(BF16)
        gate = jnp.dot(h, win_s[:, mix_w + att_w:], preferred_element_type=F32)
        gate_ref[crow, :] = _silu(gate).astype(BF16)

    y_intra = []
    for j in range(n_ssm):
        u = utb_ref[n_pool + j, cur_rows, :]
        u4 = u.reshape(n_chunk, TC, NB, LANES)
        up = jnp.concatenate([u4[:, s].reshape(n_chunk * NB, LANES) for s in range(TC)], axis=1).astype(BF16)
        yg = jnp.dot(up, wtb_ref[j], preferred_element_type=F32)
        y_intra.append(yg[:, 0:TC * LANES])
        g_ref[j] = yg[:, TC * LANES:]

    for b in range(NB):
        rows = slice(b * TT, (b + 1) * TT)
        s_ref[rows, :] = jnp.dot(q_ref[rows, :], kbd_s[b], preferred_element_type=F32)

    gw = pool_w // len(POOL_WINDOWS)
    lane = lax.broadcasted_iota(jnp.int32, (TB * NB, LANES), 1)
    for k in range(n_pool):
        w_lo = POOL_WINDOWS[(k * LANES) // gw]
        w_hi = POOL_WINDOWS[min(((k + 1) * LANES - 1) // gw, len(POOL_WINDOWS) - 1)]
        split = ((k * LANES) // gw + 1) * gw - k * LANES
        for i in range(TT // TB):
            base = (HALO + i * TB) * NB
            ext = utb_ref[k, base - (HALO - 1) * NB: base + TB * NB, :]
            sums = {1: ext}
            w = 1
            while w < w_hi:
                prev = sums[w]
                sums[2 * w] = prev[w * NB:] + prev[:-w * NB]
                w *= 2
            cur = ext[(HALO - 1) * NB:]
            lo = sums[w_lo][(HALO - w_lo) * NB:]
            hi = sums[w_hi][(HALO - w_hi) * NB:]
            ssum = jnp.where(lane < split, lo, hi)
            scale = invw_ref[:, k * LANES:(k + 1) * LANES]
            if i == 0:
                scale = jnp.where(ti == 0, invc0_ref[:, k * LANES:(k + 1) * LANES], scale)
            pd_ref[i * TB * NB:(i + 1) * TB * NB, k * LANES:(k + 1) * LANES] = (ssum * scale - cur).astype(BF16)
    for k in range(n_pool):
        utb_ref[k, 0:HALO * NB, :] = utb_ref[k, TT * NB:(TT + HALO) * NB, :]
    ypool = jnp.dot(pd_ref[...], wpool_ref[...], preferred_element_type=F32) * pscale_ref[...]
    for k in range(n_pool):
        utb_ref[k, cur_rows, :] = ypool[:, k * LANES:(k + 1) * LANES]

    a_re = [a_ref[j, :, 0:st] for j in range(n_ssm)]
    a_im = [a_ref[j, :, st:] for j in range(n_ssm)]
    h_re = [hc_ref[j, :, 0:st] for j in range(n_ssm)]
    h_im = [hc_ref[j, :, st:] for j in range(n_ssm)]
    for r in range(n_chunk):
        rows = slice(r * NB, (r + 1) * NB)
        for j in range(n_ssm):
            hin_ref[j, rows, 0:st] = h_re[j]
            hin_ref[j, rows, st:] = h_im[j]
            g_re = g_ref[j, rows, 0:st]
            g_im = g_ref[j, rows, st:]
            h_re[j], h_im[j] = (a_re[j] * h_re[j] - a_im[j] * h_im[j] + g_re,
                                a_re[j] * h_im[j] + a_im[j] * h_re[j] + g_im)
    for j in range(n_ssm):
        hc_ref[j, :, 0:st] = h_re[j]
        hc_ref[j, :, st:] = h_im[j]

    y_act = []
    for j in range(n_ssm):
        y_lin = y_intra[j] + jnp.dot(hin_ref[j].astype(BF16), wc_ref[j], preferred_element_type=F32)
        y_tb = jnp.stack([y_lin[:, s * LANES:(s + 1) * LANES].reshape(n_chunk, NB, LANES) for s in range(TC)],
                         axis=1).reshape(TT * NB, LANES)
        y = y_tb + dskip_ref[:, j * LANES:(j + 1) * LANES] * utb_ref[n_pool + j, cur_rows, :]
        y_act.append(_gelu_tanh(y).astype(BF16))

    lane_t = lax.broadcasted_iota(jnp.int32, (TT, LANES), 1)
    heads_per_vreg = LANES // hd

    def attend(b):
        rows = slice(b * TT, (b + 1) * TT)
        es, inv = [], []
        for hh in range(MEM_HEADS):
            sh = s_ref[rows, hh * n_mem:(hh + 1) * n_mem]
            e = jnp.exp2(sh - jnp.max(sh, axis=-1, keepdims=True))
            inv.append(1.0 / jnp.sum(e, axis=-1, keepdims=True))
            es.append(e.astype(BF16))
        o = jnp.dot(jnp.concatenate(es, axis=1), vbd_s[b], preferred_element_type=F32)
        norm = []
        for c in range(att_w // LANES):
            sc = inv[c * heads_per_vreg]
            for e_i in range(1, heads_per_vreg):
                sc = jnp.where(lane_t < e_i * hd, sc, inv[c * heads_per_vreg + e_i])
            norm.append(sc)
        o = o * jnp.concatenate(norm, axis=1)
        ycat_ref[rows, mix_w:] = (o * gate_ref[rows, mix_w:].astype(F32)).astype(BF16)

    for b in range(NB // 2):
        attend(b)

    z = jnp.dot(jnp.concatenate(y_act, axis=1), wglu_ref[...], preferred_element_type=F32)
    y_ssm = z[:, 0:ssm_w] * _sigmoid(z[:, ssm_w:])
    for j in range(n_ssm):
        utb_ref[n_pool + j, cur_rows, :] = y_ssm[:, j * LANES:(j + 1) * LANES]

    for b in range(NB // 2, NB):
        attend(b)

    for c in range(NB // BC):
        crow = slice(c * BC * TT, (c + 1) * BC * TT)
        for bb in range(BC):
            b = c * BC + bb
            rows = slice(b * TT, (b + 1) * TT)
            for k in range(n_slab):
                cols = slice(k * LANES, (k + 1) * LANES)
                yk = utb_ref[k, pl.ds(HALO * NB + b, TT, stride=NB), :]
                ycat_ref[rows, cols] = (yk * gate_ref[rows, cols].astype(F32)).astype(BF16)
        out = jnp.dot(ycat_ref[crow, :], wout_s[...], preferred_element_type=F32)
        ms2 = jnp.mean(out * out, axis=-1, keepdims=True)
        res = x_ref[c * BC:(c + 1) * BC].reshape(BC * TT, d) + (out * lax.rsqrt(ms2 + EPS)) * gpost_ref[...]
        out_ref[c * BC:(c + 1) * BC] = res.reshape(BC, TT, d)


def _const_spec(shape):
    zeros = (0,) * len(shape)
    return pl.BlockSpec(shape, lambda bi, ti: zeros, pipeline_mode=pl.Buffered(1))


def _layer_call(x, kt, v, g_pre, w_in, wpool_bd, pscale, inv_w, inv_c0, wtb, wc, a_mat, d_skip, w_glu, w_out, g_post,
                *, pool_w, ssm_w, att_w, n_mem):
    bsz, seq, d = x.shape
    assert bsz % NB == 0 and seq % TT == 0 and TT % TB == 0 and TT % TC == 0 and TB == HALO and NB % BC == 0
    assert pool_w % LANES == 0 and ssm_w % LANES == 0 and att_w % LANES == 0 and att_w == n_mem
    n_pool, n_ssm = pool_w // LANES, ssm_w // LANES
    n_slab = n_pool + n_ssm
    n_state = a_mat.shape[-1]
    m_rows = NB * TT
    kern = functools.partial(_layer_kernel, d=d, pool_w=pool_w, ssm_w=ssm_w, att_w=att_w, n_mem=n_mem)
    consts = (g_pre, w_in, wpool_bd, pscale, inv_w, inv_c0, wtb, wc, a_mat, d_skip, w_glu, w_out, g_post)
    in_hbm = (w_in, w_out)
    scratch = [
        ((n_slab, (HALO + TT) * NB, LANES), F32),
        ((m_rows, d), BF16),
        ((m_rows, att_w), BF16),
        ((m_rows, d), BF16),
        ((m_rows, pool_w), BF16),
        ((n_ssm, m_rows // TC, n_state), F32),
        ((n_ssm, m_rows // TC, n_state), F32),
        ((n_ssm, NB, n_state), F32),
        ((m_rows, MEM_HEADS * n_mem), F32),
        (w_in.shape, BF16),
        (w_out.shape, BF16),
        ((NB, att_w, MEM_HEADS * n_mem), BF16),
        ((NB, MEM_HEADS * n_mem, att_w), BF16),
        ((NB, att_w, n_mem), BF16),
    ]
    nbytes = lambda shape, dt: math.prod(shape) * jnp.dtype(dt).itemsize
    declared = (sum(nbytes(s, dt) for s, dt in scratch) + 2 * 2 * nbytes((NB, TT, d), x.dtype)
                + sum(nbytes(c.shape, c.dtype) for c in consts if not any(c is h for h in in_hbm)))
    vmem_limit = min(declared + VMEM_SPILL_ALLOWANCE, V7X_VMEM_BYTES - VMEM_RESERVE)
    return pl.pallas_call(
        kern,
        grid=(bsz // NB, seq // TT),
        in_specs=[
            pl.BlockSpec((NB, TT, d), lambda bi, ti: (bi, ti, 0)),
            pl.BlockSpec(memory_space=pl.ANY),
            pl.BlockSpec(memory_space=pl.ANY),
        ] + [pl.BlockSpec(memory_space=pl.ANY) if any(c is h for h in in_hbm) else _const_spec(c.shape)
             for c in consts],
        out_specs=pl.BlockSpec((NB, TT, d), lambda bi, ti: (bi, ti, 0)),
        out_shape=jax.ShapeDtypeStruct(x.shape, x.dtype),
        scratch_shapes=[pltpu.VMEM(s, dt) for s, dt in scratch],
        compiler_params=pltpu.CompilerParams(
            dimension_semantics=("arbitrary", "arbitrary"),
            vmem_limit_bytes=vmem_limit,
        ),
        name="layer_call",
    )(x, kt, v, *consts)


def _expand_kernel(xt_ref, xb_ref, xc_ref, wtb_ref, wc_ref, *, c, n_st):
    gpb = LANES // c
    tl = TC * LANES

    def rep(rows, cols, row_blk, col_blk, period):
        r = lax.broadcasted_iota(jnp.int32, (rows, cols), 0)
        q = lax.broadcasted_iota(jnp.int32, (rows, cols), 1)
        hit = (r // row_blk == q // col_blk) & (r % period == q % period)
        return jnp.where(hit, 1.0, 0.0).astype(BF16)

    def same_group(rows, cols, row_div, col_div):
        r = lax.broadcasted_iota(jnp.int32, (rows, cols), 0)
        q = lax.broadcasted_iota(jnp.int32, (rows, cols), 1)
        return (r // row_div) % gpb == (q // col_div) % gpb

    rep_t = rep(TC * c, tl, c, LANES, c)
    rep_b = rep(2 * n_st, 2 * gpb * n_st, n_st, gpb * n_st, n_st)
    wt = jnp.dot(xt_ref[0].astype(BF16), rep_t, preferred_element_type=F32)
    wtb_ref[0, :, 0:tl] = jnp.where(same_group(tl, tl, c, c), wt, 0.0).astype(BF16)
    wb = jnp.dot(xb_ref[0].astype(BF16), rep_b, preferred_element_type=F32)
    wtb_ref[0, :, tl:] = jnp.where(same_group(tl, 2 * gpb * n_st, c, n_st), wb, 0.0).astype(BF16)
    wcx = jnp.dot(xc_ref[0].astype(BF16), rep_t, preferred_element_type=F32)
    wc_ref[0] = jnp.where(same_group(2 * gpb * n_st, tl, n_st, c), wcx, 0.0).astype(BF16)


def _ssm_matrices(a_re, a_im, log_dt, b_re, b_im, c_re, c_im):
    n_grp, n_st = a_re.shape
    c = b_re.shape[-1]
    gpb = LANES // c
    nj = n_grp // gpb
    dt = jnp.exp(log_dt)[:, None]
    ks = jnp.arange(TC + 1, dtype=F32)[:, None, None]
    mag = jnp.exp(ks * (a_re * dt)[None])
    ang = ks * (a_im * dt)[None]
    p_re, p_im = mag * jnp.cos(ang), mag * jnp.sin(ang)
    den = a_re * a_re + a_im * a_im
    x_re, x_im = p_re[1] - 1.0, p_im[1]
    f_re = (x_re * a_re + x_im * a_im) / den
    f_im = (x_im * a_re - x_re * a_im) / den
    bt_re, bt_im = jnp.swapaxes(b_re, 1, 2), jnp.swapaxes(b_im, 1, 2)
    bb_re = f_re[:, None, :] * bt_re - f_im[:, None, :] * bt_im
    bb_im = f_re[:, None, :] * bt_im + f_im[:, None, :] * bt_re
    cp_re = c_re[None] * p_re[:, :, None, :] - c_im[None] * p_im[:, :, None, :]
    cp_im = c_re[None] * p_im[:, :, None, :] + c_im[None] * p_re[:, :, None, :]
    kern = jnp.sum(bb_re[None, :, :, None, :] * cp_re[:TC, :, None, :, :]
                   - bb_im[None, :, :, None, :] * cp_im[:TC, :, None, :, :], axis=-1)
    zero = jnp.zeros_like(kern[0])
    xt = jnp.stack([jnp.concatenate([kern[t - s] if t >= s else zero for t in range(TC)], axis=-1)
                    for s in range(TC)], axis=0)
    xt = xt.reshape(TC, nj, gpb * c, TC * c).swapaxes(0, 1).reshape(nj, TC * LANES, TC * c)
    q_re, q_im = p_re[TC - 1::-1][:, :, None, :], p_im[TC - 1::-1][:, :, None, :]
    xb = jnp.concatenate([q_re * bb_re[None] - q_im * bb_im[None], q_re * bb_im[None] + q_im * bb_re[None]], axis=-1)
    xb = xb.reshape(TC, nj, gpb * c, 2 * n_st).swapaxes(0, 1).reshape(nj, TC * LANES, 2 * n_st)
    xc = jnp.stack([cp_re[1:], -cp_im[1:]], axis=0)
    xc = jnp.transpose(xc, (2, 0, 4, 1, 3)).reshape(nj, gpb, 2, n_st, TC * c)
    xc = xc.swapaxes(1, 2).reshape(nj, 2 * gpb * n_st, TC * c)
    a_mat = jnp.concatenate([p_re[TC].reshape(nj, gpb * n_st), p_im[TC].reshape(nj, gpb * n_st)], axis=-1)
    a_mat = jnp.broadcast_to(a_mat[:, None, :], (nj, NB, a_mat.shape[-1]))
    n_state = 2 * gpb * n_st
    wtb, wc = pl.pallas_call(
        functools.partial(_expand_kernel, c=c, n_st=n_st),
        grid=(nj,),
        in_specs=[pl.BlockSpec((1,) + xt.shape[1:], lambda j: (j, 0, 0)),
                  pl.BlockSpec((1,) + xb.shape[1:], lambda j: (j, 0, 0)),
                  pl.BlockSpec((1,) + xc.shape[1:], lambda j: (j, 0, 0))],
        out_specs=[pl.BlockSpec((1, TC * LANES, TC * LANES + n_state), lambda j: (j, 0, 0)),
                   pl.BlockSpec((1, n_state, TC * LANES), lambda j: (j, 0, 0))],
        out_shape=[jax.ShapeDtypeStruct((nj, TC * LANES, TC * LANES + n_state), BF16),
                   jax.ShapeDtypeStruct((nj, n_state, TC * LANES), BF16)],
        compiler_params=pltpu.CompilerParams(dimension_semantics=("arbitrary",)),
        name="ssm_expand_call",
    )(xt, xb, xc)
    return wtb, wc, a_mat


def _pool_constants(w_pool, pool_w):
    n_grp, gw, _ = w_pool.shape
    wbd = (w_pool[:, :, None, :] * jnp.eye(n_grp, dtype=F32)[:, None, :, None]).reshape(pool_w, pool_w)
    win = np.repeat(np.asarray(POOL_WINDOWS, np.float32), gw)
    inv_w = jnp.asarray((1.0 / win)[None, :])
    pos = np.repeat(np.arange(1, HALO + 1, dtype=np.float32), NB)[:, None]
    inv_c0 = jnp.asarray(1.0 / np.minimum(pos, win[None, :]))
    return wbd.astype(BF16), inv_w, inv_c0


def _layer(x, mem, g_pre, w_in, w_pool, pool_scale, a_re, a_im, log_dt, b_re, b_im, c_re, c_im, d_skip, w_glu,
           g_mem, w_kv, w_out, g_post):
    d = x.shape[-1]
    n_mem = mem.shape[1]
    pool_w = w_pool.shape[0] * w_pool.shape[1]
    ssm_w = d_skip.shape[0]
    att_w = w_kv.shape[1] // 2
    assert a_re.shape[0] * SSM_GROUP == ssm_w and b_re.shape[-1] == SSM_GROUP
    assert pool_w + ssm_w + att_w == d and w_pool.shape[0] == len(POOL_WINDOWS)

    kt, v = _kv_call(mem, g_mem[None, :], w_kv[:, :att_w].astype(BF16), w_kv[:, att_w:].astype(BF16))
    wtb, wc, a_mat = _ssm_matrices(a_re, a_im, log_dt, b_re, b_im, c_re, c_im)
    wpool_bd, inv_w, inv_c0 = _pool_constants(w_pool, pool_w)
    return _layer_call(
        x, kt, v, g_pre[None, :], w_in.astype(BF16), wpool_bd, pool_scale[None, :], inv_w, inv_c0,
        wtb, wc, a_mat, d_skip[None, :], w_glu.astype(BF16), w_out.astype(BF16), g_post[None, :],
        pool_w=pool_w, ssm_w=ssm_w, att_w=att_w, n_mem=n_mem)


def kernel(x, mem, g_pre, w_in, w_pool, pool_scale, a_re, a_im, log_dt, b_re, b_im, c_re, c_im, d_skip, w_glu, g_mem, w_kv, w_out, g_post):
    for i in range(g_pre.shape[0]):
        x = _layer(x, mem, g_pre[i], w_in[i], w_pool[i], pool_scale[i], a_re[i], a_im[i], log_dt[i], b_re[i], b_im[i],
                   c_re[i], c_im[i], d_skip[i], w_glu[i], g_mem[i], w_kv[i], w_out[i], g_post[i])
    return x
```

```python
import functools
import math

import jax
import jax.numpy as jnp
import numpy as np
from jax import lax
from jax.experimental import pallas as pl
from jax.experimental.pallas import tpu as pltpu

LANES = 128
NB = 8
TT = 128
TC = 4
BC = 4
TB = 16
KV_MB = 8
HALO = 16
POOL_WINDOWS = (2, 4, 8, 16)
SSM_GROUP = 16
MEM_HEADS = 4
EPS = 1e-6
LOG2_E = 1.4426950408889634
GELU_C = 0.7978845608028654
V7X_VMEM_BYTES = 64 * 1024 * 1024
VMEM_RESERVE = 4 * 1024 * 1024
VMEM_SPILL_ALLOWANCE = 8 * 1024 * 1024

F32 = jnp.float32
BF16 = jnp.bfloat16
FP8 = jnp.float8_e4m3fn
P_SHIFT = 8.0


def _sigmoid(v):
    return 0.5 * jnp.tanh(0.5 * v) + 0.5


def _silu(v):
    hv = 0.5 * v
    return hv + hv * jnp.tanh(hv)


def _gelu_tanh(v):
    hv = 0.5 * v
    return hv + hv * jnp.tanh(v * (GELU_C + (GELU_C * 0.044715) * (v * v)))


def _kv_kernel(mem_ref, gmem_ref, wk_ref, wv_ref, kt_ref, v_ref, *, att_w, n_mem):
    mb, _, d = mem_ref.shape
    m = mem_ref[...].reshape(mb * n_mem, d)
    ms = jnp.mean(m * m, axis=-1, keepdims=True)
    mn = ((m * lax.rsqrt(ms + EPS)) * gmem_ref[...]).astype(BF16)
    kt = lax.dot_general(wk_ref[...], mn, (((0,), (1,)), ((), ())), preferred_element_type=F32)
    v = jnp.dot(mn, wv_ref[...], preferred_element_type=F32)
    for i in range(mb):
        kt_ref[i] = kt[:, i * n_mem:(i + 1) * n_mem].astype(BF16)
        v_ref[i] = v[i * n_mem:(i + 1) * n_mem, :].astype(BF16)


def _kv_call(mem, g_mem, wk, wv):
    bsz, n_mem, d = mem.shape
    att_w = wv.shape[1]
    assert bsz % KV_MB == 0
    kern = functools.partial(_kv_kernel, att_w=att_w, n_mem=n_mem)
    return pl.pallas_call(
        kern,
        grid=(bsz // KV_MB,),
        in_specs=[
            pl.BlockSpec((KV_MB, n_mem, d), lambda b: (b, 0, 0)),
            pl.BlockSpec((1, d), lambda b: (0, 0)),
            pl.BlockSpec((d, att_w), lambda b: (0, 0)),
            pl.BlockSpec((d, att_w), lambda b: (0, 0)),
        ],
        out_specs=[
            pl.BlockSpec((KV_MB, att_w, n_mem), lambda b: (b, 0, 0)),
            pl.BlockSpec((KV_MB, n_mem, att_w), lambda b: (b, 0, 0)),
        ],
        out_shape=[
            jax.ShapeDtypeStruct((bsz, att_w, n_mem), BF16),
            jax.ShapeDtypeStruct((bsz, n_mem, att_w), BF16),
        ],
        compiler_params=pltpu.CompilerParams(dimension_semantics=("arbitrary",)),
        name="kv_call",
    )(mem, g_mem, wk, wv)


def _layer_kernel(x_ref, kt_ref, v_ref, gpre_ref, win_ref, wpool_ref, pscale_ref, invw_ref, invc0_ref,
                  wtb_ref, wc_ref, a_ref, dskip_ref, wglu_ref, wout_ref, gpost_ref,
                  out_ref,
                  utb_ref, gate_ref, q_ref, ycat_ref, pd_ref, g_ref, hin_ref, hc_ref,
                  s_ref, win_s, wout_s, kbd_s, vbd_s, kv_stage,
                  *, d, pool_w, ssm_w, att_w, n_mem):
    bi = pl.program_id(0)
    ti = pl.program_id(1)
    n_pool = pool_w // LANES
    n_ssm = ssm_w // LANES
    n_slab = n_pool + n_ssm
    mix_w = pool_w + ssm_w
    n_chunk = TT // TC
    st = hc_ref.shape[-1] // 2
    hd = att_w // MEM_HEADS
    cur_rows = slice(HALO * NB, (HALO + TT) * NB)

    @pl.when((bi == 0) & (ti == 0))
    def _():
        pltpu.sync_copy(win_ref, win_s)
        pltpu.sync_copy(wout_ref, wout_s)

    @pl.when(ti == 0)
    def _():
        group = pl.ds(pl.multiple_of(bi * NB, NB), NB)
        row_head = lax.broadcasted_iota(jnp.int32, (att_w, n_mem), 0) // hd
        col_head = lax.broadcasted_iota(jnp.int32, (n_mem, att_w), 1) // hd
        pltpu.sync_copy(kt_ref.at[group], kv_stage)
        for b in range(NB):
            kt_b = kv_stage[b].astype(F32)
            for hh in range(MEM_HEADS):
                kbd_s[b, :, hh * n_mem:(hh + 1) * n_mem] = jnp.where(row_head == hh, kt_b, 0.0).astype(FP8)
        pltpu.sync_copy(v_ref.at[group], kv_stage)
        for b in range(NB):
            v_b = kv_stage[b].astype(F32)
            for hh in range(MEM_HEADS):
                vbd_s[b, hh * n_mem:(hh + 1) * n_mem, :] = jnp.where(col_head == hh, v_b, 0.0).astype(FP8)
        utb_ref[0:n_pool, 0:HALO * NB, :] = jnp.zeros((n_pool, HALO * NB, LANES), F32)
        hc_ref[...] = jnp.zeros(hc_ref.shape, F32)

    for c in range(NB // BC):
        crow = slice(c * BC * TT, (c + 1) * BC * TT)
        x = x_ref[c * BC:(c + 1) * BC].reshape(BC * TT, d)
        ms = jnp.mean(x * x, axis=-1, keepdims=True)
        h = ((x * lax.rsqrt(ms + EPS)) * gpre_ref[...]).astype(BF16)
        val = jnp.dot(h, win_s[:, 0:mix_w], preferred_element_type=F32)
        for bb in range(BC):
            for k in range(n_slab):
                utb_ref[k, pl.ds(HALO * NB + c * BC + bb, TT, stride=NB), :] = (
                    val[bb * TT:(bb + 1) * TT, k * LANES:(k + 1) * LANES])
        q_ref[crow, :] = jnp.dot(h, win_s[:, mix_w:mix_w + att_w], preferred_element_type=F32).astype(FP8)
        gate = jnp.dot(h, win_s[:, mix_w + att_w:], preferred_element_type=F32)
        gate_ref[crow, :] = _silu(gate).astype(BF16)

    y_intra = []
    for j in range(n_ssm):
        u = utb_ref[n_pool + j, cur_rows, :]
        u4 = u.reshape(n_chunk, TC, NB, LANES)
        up = jnp.concatenate([u4[:, s].reshape(n_chunk * NB, LANES) for s in range(TC)], axis=1).astype(BF16)
        yg = jnp.dot(up, wtb_ref[j], preferred_element_type=F32)
        y_intra.append(yg[:, 0:TC * LANES])
        g_ref[j] = yg[:, TC * LANES:]

    for b in range(NB):
        rows = slice(b * TT, (b + 1) * TT)
        s_ref[rows, :] = jnp.dot(q_ref[rows, :], kbd_s[b], preferred_element_type=F32)

    gw = pool_w // len(POOL_WINDOWS)
    lane = lax.broadcasted_iota(jnp.int32, (TB * NB, LANES), 1)
    for k in range(n_pool):
        w_lo = POOL_WINDOWS[(k * LANES) // gw]
        w_hi = POOL_WINDOWS[min(((k + 1) * LANES - 1) // gw, len(POOL_WINDOWS) - 1)]
        split = ((k * LANES) // gw + 1) * gw - k * LANES
        for i in range(TT // TB):
            base = (HALO + i * TB) * NB
            ext = utb_ref[k, base - (HALO - 1) * NB: base + TB * NB, :]
            sums = {1: ext}
            w = 1
            while w < w_hi:
                prev = sums[w]
                sums[2 * w] = prev[w * NB:] + prev[:-w * NB]
                w *= 2
            cur = ext[(HALO - 1) * NB:]
            lo = sums[w_lo][(HALO - w_lo) * NB:]
            hi = sums[w_hi][(HALO - w_hi) * NB:]
            ssum = jnp.where(lane < split, lo, hi)
            scale = invw_ref[:, k * LANES:(k + 1) * LANES]
            if i == 0:
                scale = jnp.where(ti == 0, invc0_ref[:, k * LANES:(k + 1) * LANES], scale)
            pd_ref[i * TB * NB:(i + 1) * TB * NB, k * LANES:(k + 1) * LANES] = (ssum * scale - cur).astype(BF16)
    for k in range(n_pool):
        utb_ref[k, 0:HALO * NB, :] = utb_ref[k, TT * NB:(TT + HALO) * NB, :]
    ypool = jnp.dot(pd_ref[...], wpool_ref[...], preferred_element_type=F32) * pscale_ref[...]
    for k in range(n_pool):
        utb_ref[k, cur_rows, :] = ypool[:, k * LANES:(k + 1) * LANES]

    a_re = [a_ref[j, :, 0:st] for j in range(n_ssm)]
    a_im = [a_ref[j, :, st:] for j in range(n_ssm)]
    h_re = [hc_ref[j, :, 0:st] for j in range(n_ssm)]
    h_im = [hc_ref[j, :, st:] for j in range(n_ssm)]
    for r in range(n_chunk):
        rows = slice(r * NB, (r + 1) * NB)
        for j in range(n_ssm):
            hin_ref[j, rows, 0:st] = h_re[j]
            hin_ref[j, rows, st:] = h_im[j]
            g_re = g_ref[j, rows, 0:st]
            g_im = g_ref[j, rows, st:]
            h_re[j], h_im[j] = (a_re[j] * h_re[j] - a_im[j] * h_im[j] + g_re,
                                a_re[j] * h_im[j] + a_im[j] * h_re[j] + g_im)
    for j in range(n_ssm):
        hc_ref[j, :, 0:st] = h_re[j]
        hc_ref[j, :, st:] = h_im[j]

    y_act = []
    for j in range(n_ssm):
        y_lin = y_intra[j] + jnp.dot(hin_ref[j].astype(BF16), wc_ref[j], preferred_element_type=F32)
        y_tb = jnp.stack([y_lin[:, s * LANES:(s + 1) * LANES].reshape(n_chunk, NB, LANES) for s in range(TC)],
                         axis=1).reshape(TT * NB, LANES)
        y = y_tb + dskip_ref[:, j * LANES:(j + 1) * LANES] * utb_ref[n_pool + j, cur_rows, :]
        y_act.append(_gelu_tanh(y).astype(BF16))

    lane_t = lax.broadcasted_iota(jnp.int32, (TT, LANES), 1)
    heads_per_vreg = LANES // hd

    def attend(b):
        rows = slice(b * TT, (b + 1) * TT)
        es, inv = [], []
        for hh in range(MEM_HEADS):
            sh = s_ref[rows, hh * n_mem:(hh + 1) * n_mem]
            off = P_SHIFT - jnp.max(sh, axis=-1, keepdims=True) * (hd ** -0.5 * LOG2_E)
            e = jnp.exp2(sh * (hd ** -0.5 * LOG2_E) + off)
            inv.append(1.0 / jnp.sum(e, axis=-1, keepdims=True))
            es.append(e.astype(FP8))
        o = jnp.dot(jnp.concatenate(es, axis=1), vbd_s[b], preferred_element_type=F32)
        norm = []
        for c in range(att_w // LANES):
            sc = inv[c * heads_per_vreg]
            for e_i in range(1, heads_per_vreg):
                sc = jnp.where(lane_t < e_i * hd, sc, inv[c * heads_per_vreg + e_i])
            norm.append(sc)
        o = o * jnp.concatenate(norm, axis=1)
        ycat_ref[rows, mix_w:] = (o * gate_ref[rows, mix_w:].astype(F32)).astype(BF16)

    for b in range(NB // 2):
        attend(b)

    z = jnp.dot(jnp.concatenate(y_act, axis=1), wglu_ref[...], preferred_element_type=F32)
    y_ssm = z[:, 0:ssm_w] * _sigmoid(z[:, ssm_w:])
    for j in range(n_ssm):
        utb_ref[n_pool + j, cur_rows, :] = y_ssm[:, j * LANES:(j + 1) * LANES]

    for b in range(NB // 2, NB):
        attend(b)

    for c in range(NB // BC):
        crow = slice(c * BC * TT, (c + 1) * BC * TT)
        for bb in range(BC):
            b = c * BC + bb
            rows = slice(b * TT, (b + 1) * TT)
            for k in range(n_slab):
                cols = slice(k * LANES, (k + 1) * LANES)
                yk = utb_ref[k, pl.ds(HALO * NB + b, TT, stride=NB), :]
                ycat_ref[rows, cols] = (yk * gate_ref[rows, cols].astype(F32)).astype(BF16)
        out = jnp.dot(ycat_ref[crow, :], wout_s[...], preferred_element_type=F32)
        ms2 = jnp.mean(out * out, axis=-1, keepdims=True)
        res = x_ref[c * BC:(c + 1) * BC].reshape(BC * TT, d) + (out * lax.rsqrt(ms2 + EPS)) * gpost_ref[...]
        out_ref[c * BC:(c + 1) * BC] = res.reshape(BC, TT, d)


def _const_spec(shape):
    zeros = (0,) * len(shape)
    return pl.BlockSpec(shape, lambda bi, ti: zeros, pipeline_mode=pl.Buffered(1))


def _layer_call(x, kt, v, g_pre, w_in, wpool_bd, pscale, inv_w, inv_c0, wtb, wc, a_mat, d_skip, w_glu, w_out, g_post,
                *, pool_w, ssm_w, att_w, n_mem):
    bsz, seq, d = x.shape
    assert bsz % NB == 0 and seq % TT == 0 and TT % TB == 0 and TT % TC == 0 and TB == HALO and NB % BC == 0
    assert pool_w % LANES == 0 and ssm_w % LANES == 0 and att_w % LANES == 0 and att_w == n_mem
    n_pool, n_ssm = pool_w // LANES, ssm_w // LANES
    n_slab = n_pool + n_ssm
    n_state = a_mat.shape[-1]
    m_rows = NB * TT
    kern = functools.partial(_layer_kernel, d=d, pool_w=pool_w, ssm_w=ssm_w, att_w=att_w, n_mem=n_mem)
    consts = (g_pre, w_in, wpool_bd, pscale, inv_w, inv_c0, wtb, wc, a_mat, d_skip, w_glu, w_out, g_post)
    in_hbm = (w_in, w_out)
    scratch = [
        ((n_slab, (HALO + TT) * NB, LANES), F32),
        ((m_rows, d), BF16),
        ((m_rows, att_w), FP8),
        ((m_rows, d), BF16),
        ((m_rows, pool_w), BF16),
        ((n_ssm, m_rows // TC, n_state), F32),
        ((n_ssm, m_rows // TC, n_state), F32),
        ((n_ssm, NB, n_state), F32),
        ((m_rows, MEM_HEADS * n_mem), F32),
        (w_in.shape, BF16),
        (w_out.shape, BF16),
        ((NB, att_w, MEM_HEADS * n_mem), FP8),
        ((NB, MEM_HEADS * n_mem, att_w), FP8),
        ((NB, att_w, n_mem), BF16),
    ]
    nbytes = lambda shape, dt: math.prod(shape) * jnp.dtype(dt).itemsize
    declared = (sum(nbytes(s, dt) for s, dt in scratch) + 2 * 2 * nbytes((NB, TT, d), x.dtype)
                + sum(nbytes(c.shape, c.dtype) for c in consts if not any(c is h for h in in_hbm)))
    vmem_limit = min(declared + VMEM_SPILL_ALLOWANCE, V7X_VMEM_BYTES - VMEM_RESERVE)
    return pl.pallas_call(
        kern,
        grid=(bsz // NB, seq // TT),
        in_specs=[
            pl.BlockSpec((NB, TT, d), lambda bi, ti: (bi, ti, 0)),
            pl.BlockSpec(memory_space=pl.ANY),
            pl.BlockSpec(memory_space=pl.ANY),
        ] + [pl.BlockSpec(memory_space=pl.ANY) if any(c is h for h in in_hbm) else _const_spec(c.shape)
             for c in consts],
        out_specs=pl.BlockSpec((NB, TT, d), lambda bi, ti: (bi, ti, 0)),
        out_shape=jax.ShapeDtypeStruct(x.shape, x.dtype),
        scratch_shapes=[pltpu.VMEM(s, dt) for s, dt in scratch],
        compiler_params=pltpu.CompilerParams(
            dimension_semantics=("arbitrary", "arbitrary"),
            vmem_limit_bytes=vmem_limit,
        ),
        name="layer_call",
    )(x, kt, v, *consts)


def _expand_kernel(xt_ref, xb_ref, xc_ref, wtb_ref, wc_ref, *, c, n_st):
    gpb = LANES // c
    tl = TC * LANES

    def rep(rows, cols, row_blk, col_blk, period):
        r = lax.broadcasted_iota(jnp.int32, (rows, cols), 0)
        q = lax.broadcasted_iota(jnp.int32, (rows, cols), 1)
        hit = (r // row_blk == q // col_blk) & (r % period == q % period)
        return jnp.where(hit, 1.0, 0.0).astype(BF16)

    def same_group(rows, cols, row_div, col_div):
        r = lax.broadcasted_iota(jnp.int32, (rows, cols), 0)
        q = lax.broadcasted_iota(jnp.int32, (rows, cols), 1)
        return (r // row_div) % gpb == (q // col_div) % gpb

    rep_t = rep(TC * c, tl, c, LANES, c)
    rep_b = rep(2 * n_st, 2 * gpb * n_st, n_st, gpb * n_st, n_st)
    wt = jnp.dot(xt_ref[0].astype(BF16), rep_t, preferred_element_type=F32)
    wtb_ref[0, :, 0:tl] = jnp.where(same_group(tl, tl, c, c), wt, 0.0).astype(BF16)
    wb = jnp.dot(xb_ref[0].astype(BF16), rep_b, preferred_element_type=F32)
    wtb_ref[0, :, tl:] = jnp.where(same_group(tl, 2 * gpb * n_st, c, n_st), wb, 0.0).astype(BF16)
    wcx = jnp.dot(xc_ref[0].astype(BF16), rep_t, preferred_element_type=F32)
    wc_ref[0] = jnp.where(same_group(2 * gpb * n_st, tl, n_st, c), wcx, 0.0).astype(BF16)


def _ssm_matrices(a_re, a_im, log_dt, b_re, b_im, c_re, c_im):
    n_grp, n_st = a_re.shape
    c = b_re.shape[-1]
    gpb = LANES // c
    nj = n_grp // gpb
    dt = jnp.exp(log_dt)[:, None]
    ks = jnp.arange(TC + 1, dtype=F32)[:, None, None]
    mag = jnp.exp(ks * (a_re * dt)[None])
    ang = ks * (a_im * dt)[None]
    p_re, p_im = mag * jnp.cos(ang), mag * jnp.sin(ang)
    den = a_re * a_re + a_im * a_im
    x_re, x_im = p_re[1] - 1.0, p_im[1]
    f_re = (x_re * a_re + x_im * a_im) / den
    f_im = (x_im * a_re - x_re * a_im) / den
    bt_re, bt_im = jnp.swapaxes(b_re, 1, 2), jnp.swapaxes(b_im, 1, 2)
    bb_re = f_re[:, None, :] * bt_re - f_im[:, None, :] * bt_im
    bb_im = f_re[:, None, :] * bt_im + f_im[:, None, :] * bt_re
    cp_re = c_re[None] * p_re[:, :, None, :] - c_im[None] * p_im[:, :, None, :]
    cp_im = c_re[None] * p_im[:, :, None, :] + c_im[None] * p_re[:, :, None, :]
    kern = jnp.sum(bb_re[None, :, :, None, :] * cp_re[:TC, :, None, :, :]
                   - bb_im[None, :, :, None, :] * cp_im[:TC, :, None, :, :], axis=-1)
    zero = jnp.zeros_like(kern[0])
    xt = jnp.stack([jnp.concatenate([kern[t - s] if t >= s else zero for t in range(TC)], axis=-1)
                    for s in range(TC)], axis=0)
    xt = xt.reshape(TC, nj, gpb * c, TC * c).swapaxes(0, 1).reshape(nj, TC * LANES, TC * c)
    q_re, q_im = p_re[TC - 1::-1][:, :, None, :], p_im[TC - 1::-1][:, :, None, :]
    xb = jnp.concatenate([q_re * bb_re[None] - q_im * bb_im[None], q_re * bb_im[None] + q_im * bb_re[None]], axis=-1)
    xb = xb.reshape(TC, nj, gpb * c, 2 * n_st).swapaxes(0, 1).reshape(nj, TC * LANES, 2 * n_st)
    xc = jnp.stack([cp_re[1:], -cp_im[1:]], axis=0)
    xc = jnp.transpose(xc, (2, 0, 4, 1, 3)).reshape(nj, gpb, 2, n_st, TC * c)
    xc = xc.swapaxes(1, 2).reshape(nj, 2 * gpb * n_st, TC * c)
    a_mat = jnp.concatenate([p_re[TC].reshape(nj, gpb * n_st), p_im[TC].reshape(nj, gpb * n_st)], axis=-1)
    a_mat = jnp.broadcast_to(a_mat[:, None, :], (nj, NB, a_mat.shape[-1]))
    n_state = 2 * gpb * n_st
    wtb, wc = pl.pallas_call(
        functools.partial(_expand_kernel, c=c, n_st=n_st),
        grid=(nj,),
        in_specs=[pl.BlockSpec((1,) + xt.shape[1:], lambda j: (j, 0, 0)),
                  pl.BlockSpec((1,) + xb.shape[1:], lambda j: (j, 0, 0)),
                  pl.BlockSpec((1,) + xc.shape[1:], lambda j: (j, 0, 0))],
        out_specs=[pl.BlockSpec((1, TC * LANES, TC * LANES + n_state), lambda j: (j, 0, 0)),
                   pl.BlockSpec((1, n_state, TC * LANES), lambda j: (j, 0, 0))],
        out_shape=[jax.ShapeDtypeStruct((nj, TC * LANES, TC * LANES + n_state), BF16),
                   jax.ShapeDtypeStruct((nj, n_state, TC * LANES), BF16)],
        compiler_params=pltpu.CompilerParams(dimension_semantics=("arbitrary",)),
        name="ssm_expand_call",
    )(xt, xb, xc)
    return wtb, wc, a_mat


def _pool_constants(w_pool, pool_w):
    n_grp, gw, _ = w_pool.shape
    wbd = (w_pool[:, :, None, :] * jnp.eye(n_grp, dtype=F32)[:, None, :, None]).reshape(pool_w, pool_w)
    win = np.repeat(np.asarray(POOL_WINDOWS, np.float32), gw)
    inv_w = jnp.asarray((1.0 / win)[None, :])
    pos = np.repeat(np.arange(1, HALO + 1, dtype=np.float32), NB)[:, None]
    inv_c0 = jnp.asarray(1.0 / np.minimum(pos, win[None, :]))
    return wbd.astype(BF16), inv_w, inv_c0


def _layer(x, mem, g_pre, w_in, w_pool, pool_scale, a_re, a_im, log_dt, b_re, b_im, c_re, c_im, d_skip, w_glu,
           g_mem, w_kv, w_out, g_post):
    d = x.shape[-1]
    n_mem = mem.shape[1]
    pool_w = w_pool.shape[0] * w_pool.shape[1]
    ssm_w = d_skip.shape[0]
    att_w = w_kv.shape[1] // 2
    assert a_re.shape[0] * SSM_GROUP == ssm_w and b_re.shape[-1] == SSM_GROUP
    assert pool_w + ssm_w + att_w == d and w_pool.shape[0] == len(POOL_WINDOWS)

    kt, v = _kv_call(mem, g_mem[None, :], w_kv[:, :att_w].astype(BF16), w_kv[:, att_w:].astype(BF16))
    wtb, wc, a_mat = _ssm_matrices(a_re, a_im, log_dt, b_re, b_im, c_re, c_im)
    wpool_bd, inv_w, inv_c0 = _pool_constants(w_pool, pool_w)
    return _layer_call(
        x, kt, v, g_pre[None, :], w_in.astype(BF16), wpool_bd, pool_scale[None, :], inv_w, inv_c0,
        wtb, wc, a_mat, d_skip[None, :], w_glu.astype(BF16), w_out.astype(BF16), g_post[None, :],
        pool_w=pool_w, ssm_w=ssm_w, att_w=att_w, n_mem=n_mem)


def kernel(x, mem, g_pre, w_in, w_pool, pool_scale, a_re, a_im, log_dt, b_re, b_im, c_re, c_im, d_skip, w_glu, g_mem, w_kv, w_out, g_post):
    for i in range(g_pre.shape[0]):
        x = _layer(x, mem, g_pre[i], w_in[i], w_pool[i], pool_scale[i], a_re[i], a_im[i], log_dt[i], b_re[i], b_im[i],
                   c_re[i], c_im[i], d_skip[i], w_glu[i], g_mem[i], w_kv[i], w_out[i], g_post[i])
    return x
```

```python
import functools
import math

import jax
import jax.numpy as jnp
import numpy as np
from jax import lax
from jax.experimental import pallas as pl
from jax.experimental.pallas import tpu as pltpu

LANES = 128
NB = 8
TT = 128
TC = 4
BC = 4
TB = 16
KV_MB = 8
HALO = 16
POOL_WINDOWS = (2, 4, 8, 16)
SSM_GROUP = 16
MEM_HEADS = 4
EPS = 1e-6
LOG2_E = 1.4426950408889634
GELU_C = 0.7978845608028654
V7X_VMEM_BYTES = 64 * 1024 * 1024
VMEM_RESERVE = 4 * 1024 * 1024
VMEM_SPILL_ALLOWANCE = 8 * 1024 * 1024

F32 = jnp.float32
BF16 = jnp.bfloat16
FP8 = jnp.float8_e4m3fn


def _sigmoid(v):
    return 0.5 * jnp.tanh(0.5 * v) + 0.5


def _silu(v):
    hv = 0.5 * v
    return hv + hv * jnp.tanh(hv)


def _gelu_tanh(v):
    hv = 0.5 * v
    return hv + hv * jnp.tanh(v * (GELU_C + (GELU_C * 0.044715) * (v * v)))


def _kv_kernel(mem_ref, gmem_ref, wk_ref, wv_ref, kt_ref, v_ref, *, att_w, n_mem):
    mb, _, d = mem_ref.shape
    m = mem_ref[...].reshape(mb * n_mem, d)
    ms = jnp.mean(m * m, axis=-1, keepdims=True)
    mn = ((m * lax.rsqrt(ms + EPS)) * gmem_ref[...]).astype(BF16)
    kt = lax.dot_general(wk_ref[...], mn, (((0,), (1,)), ((), ())), preferred_element_type=F32)
    v = jnp.dot(mn, wv_ref[...], preferred_element_type=F32)
    hd = att_w // MEM_HEADS
    kt = kt * (hd ** -0.5 * LOG2_E)
    for i in range(mb):
        kt_ref[i] = kt[:, i * n_mem:(i + 1) * n_mem].astype(BF16)
        v_ref[i] = v[i * n_mem:(i + 1) * n_mem, :].astype(BF16)


def _kv_call(mem, g_mem, wk, wv):
    bsz, n_mem, d = mem.shape
    att_w = wv.shape[1]
    assert bsz % KV_MB == 0
    kern = functools.partial(_kv_kernel, att_w=att_w, n_mem=n_mem)
    return pl.pallas_call(
        kern,
        grid=(bsz // KV_MB,),
        in_specs=[
            pl.BlockSpec((KV_MB, n_mem, d), lambda b: (b, 0, 0)),
            pl.BlockSpec((1, d), lambda b: (0, 0)),
            pl.BlockSpec((d, att_w), lambda b: (0, 0)),
            pl.BlockSpec((d, att_w), lambda b: (0, 0)),
        ],
        out_specs=[
            pl.BlockSpec((KV_MB, att_w, n_mem), lambda b: (b, 0, 0)),
            pl.BlockSpec((KV_MB, n_mem, att_w), lambda b: (b, 0, 0)),
        ],
        out_shape=[
            jax.ShapeDtypeStruct((bsz, att_w, n_mem), BF16),
            jax.ShapeDtypeStruct((bsz, n_mem, att_w), BF16),
        ],
        compiler_params=pltpu.CompilerParams(dimension_semantics=("arbitrary",)),
        name="kv_call",
    )(mem, g_mem, wk, wv)


def _layer_kernel(x_ref, kt_ref, v_ref, gpre_ref, win_ref, wpool_ref, pscale_ref, invw_ref, invc0_ref,
                  wtb_ref, wc_ref, a_ref, dskip_ref, wglu_ref, wout_ref, gpost_ref,
                  out_ref,
                  utb_ref, gate_ref, q_ref, ycat_ref, pd_ref, g_ref, hin_ref, hc_ref,
                  s_ref, win_s, wout_s, kbd_s, vbd_s, kv_stage,
                  *, d, pool_w, ssm_w, att_w, n_mem):
    bi = pl.program_id(0)
    ti = pl.program_id(1)
    n_pool = pool_w // LANES
    n_ssm = ssm_w // LANES
    n_slab = n_pool + n_ssm
    mix_w = pool_w + ssm_w
    n_chunk = TT // TC
    st = hc_ref.shape[-1] // 2
    hd = att_w // MEM_HEADS
    cur_rows = slice(HALO * NB, (HALO + TT) * NB)

    @pl.when((bi == 0) & (ti == 0))
    def _():
        pltpu.sync_copy(win_ref, win_s)
        pltpu.sync_copy(wout_ref, wout_s)

    @pl.when(ti == 0)
    def _():
        group = pl.ds(pl.multiple_of(bi * NB, NB), NB)
        row_head = lax.broadcasted_iota(jnp.int32, (att_w, n_mem), 0) // hd
        col_head = lax.broadcasted_iota(jnp.int32, (n_mem, att_w), 1) // hd
        pltpu.sync_copy(kt_ref.at[group], kv_stage)
        for b in range(NB):
            kt_b = kv_stage[b].astype(F32)
            for hh in range(MEM_HEADS):
                kbd_s[b, :, hh * n_mem:(hh + 1) * n_mem] = jnp.where(row_head == hh, kt_b, 0.0).astype(FP8)
        pltpu.sync_copy(v_ref.at[group], kv_stage)
        for b in range(NB):
            v_b = kv_stage[b].astype(F32)
            for hh in range(MEM_HEADS):
                vbd_s[b, hh * n_mem:(hh + 1) * n_mem, :] = jnp.where(col_head == hh, v_b, 0.0).astype(FP8)
        utb_ref[0:n_pool, 0:HALO * NB, :] = jnp.zeros((n_pool, HALO * NB, LANES), F32)
        hc_ref[...] = jnp.zeros(hc_ref.shape, F32)

    for c in range(NB // BC):
        crow = slice(c * BC * TT, (c + 1) * BC * TT)
        x = x_ref[c * BC:(c + 1) * BC].reshape(BC * TT, d)
        ms = jnp.mean(x * x, axis=-1, keepdims=True)
        h = ((x * lax.rsqrt(ms + EPS)) * gpre_ref[...]).astype(BF16)
        val = jnp.dot(h, win_s[:, 0:mix_w], preferred_element_type=F32)
        for bb in range(BC):
            for k in range(n_slab):
                utb_ref[k, pl.ds(HALO * NB + c * BC + bb, TT, stride=NB), :] = (
                    val[bb * TT:(bb + 1) * TT, k * LANES:(k + 1) * LANES])
        q_ref[crow, :] = jnp.dot(h, win_s[:, mix_w:mix_w + att_w], preferred_element_type=F32).astype(BF16)
        gate = jnp.dot(h, win_s[:, mix_w + att_w:], preferred_element_type=F32)
        gate_ref[crow, :] = _silu(gate).astype(BF16)

    y_intra = []
    for j in range(n_ssm):
        u = utb_ref[n_pool + j, cur_rows, :]
        u4 = u.reshape(n_chunk, TC, NB, LANES)
        up = jnp.concatenate([u4[:, s].reshape(n_chunk * NB, LANES) for s in range(TC)], axis=1).astype(BF16)
        yg = jnp.dot(up, wtb_ref[j], preferred_element_type=F32)
        y_intra.append(yg[:, 0:TC * LANES])
        g_ref[j] = yg[:, TC * LANES:]

    for b in range(NB):
        rows = slice(b * TT, (b + 1) * TT)
        s_ref[rows, :] = jnp.dot(q_ref[rows, :], kbd_s[b], preferred_element_type=F32)

    gw = pool_w // len(POOL_WINDOWS)
    lane = lax.broadcasted_iota(jnp.int32, (TB * NB, LANES), 1)
    for k in range(n_pool):
        w_lo = POOL_WINDOWS[(k * LANES) // gw]
        w_hi = POOL_WINDOWS[min(((k + 1) * LANES - 1) // gw, len(POOL_WINDOWS) - 1)]
        split = ((k * LANES) // gw + 1) * gw - k * LANES
        for i in range(TT // TB):
            base = (HALO + i * TB) * NB
            ext = utb_ref[k, base - (HALO - 1) * NB: base + TB * NB, :]
            sums = {1: ext}
            w = 1
            while w < w_hi:
                prev = sums[w]
                sums[2 * w] = prev[w * NB:] + prev[:-w * NB]
                w *= 2
            cur = ext[(HALO - 1) * NB:]
            lo = sums[w_lo][(HALO - w_lo) * NB:]
            hi = sums[w_hi][(HALO - w_hi) * NB:]
            ssum = jnp.where(lane < split, lo, hi)
            scale = invw_ref[:, k * LANES:(k + 1) * LANES]
            if i == 0:
                scale = jnp.where(ti == 0, invc0_ref[:, k * LANES:(k + 1) * LANES], scale)
            pd_ref[i * TB * NB:(i + 1) * TB * NB, k * LANES:(k + 1) * LANES] = (ssum * scale - cur).astype(BF16)
    for k in range(n_pool):
        utb_ref[k, 0:HALO * NB, :] = utb_ref[k, TT * NB:(TT + HALO) * NB, :]
    ypool = jnp.dot(pd_ref[...], wpool_ref[...], preferred_element_type=F32) * pscale_ref[...]
    for k in range(n_pool):
        utb_ref[k, cur_rows, :] = ypool[:, k * LANES:(k + 1) * LANES]

    a_re = [a_ref[j, :, 0:st] for j in range(n_ssm)]
    a_im = [a_ref[j, :, st:] for j in range(n_ssm)]
    h_re = [hc_ref[j, :, 0:st] for j in range(n_ssm)]
    h_im = [hc_ref[j, :, st:] for j in range(n_ssm)]
    for r in range(n_chunk):
        rows = slice(r * NB, (r + 1) * NB)
        for j in range(n_ssm):
            hin_ref[j, rows, 0:st] = h_re[j]
            hin_ref[j, rows, st:] = h_im[j]
            g_re = g_ref[j, rows, 0:st]
            g_im = g_ref[j, rows, st:]
            h_re[j], h_im[j] = (a_re[j] * h_re[j] - a_im[j] * h_im[j] + g_re,
                                a_re[j] * h_im[j] + a_im[j] * h_re[j] + g_im)
    for j in range(n_ssm):
        hc_ref[j, :, 0:st] = h_re[j]
        hc_ref[j, :, st:] = h_im[j]

    y_act = []
    for j in range(n_ssm):
        y_lin = y_intra[j] + jnp.dot(hin_ref[j].astype(BF16), wc_ref[j], preferred_element_type=F32)
        y_tb = jnp.stack([y_lin[:, s * LANES:(s + 1) * LANES].reshape(n_chunk, NB, LANES) for s in range(TC)],
                         axis=1).reshape(TT * NB, LANES)
        y = y_tb + dskip_ref[:, j * LANES:(j + 1) * LANES] * utb_ref[n_pool + j, cur_rows, :]
        y_act.append(_gelu_tanh(y).astype(BF16))

    lane_t = lax.broadcasted_iota(jnp.int32, (TT, LANES), 1)
    heads_per_vreg = LANES // hd

    def attend(b):
        rows = slice(b * TT, (b + 1) * TT)
        es, inv = [], []
        for hh in range(MEM_HEADS):
            sh = s_ref[rows, hh * n_mem:(hh + 1) * n_mem]
            e = jnp.exp2(sh - jnp.max(sh, axis=-1, keepdims=True))
            inv.append(1.0 / jnp.sum(e, axis=-1, keepdims=True))
            es.append(e.astype(BF16))
        o = jnp.dot(jnp.concatenate(es, axis=1), vbd_s[b], preferred_element_type=F32)
        norm = []
        for c in range(att_w // LANES):
            sc = inv[c * heads_per_vreg]
            for e_i in range(1, heads_per_vreg):
                sc = jnp.where(lane_t < e_i * hd, sc, inv[c * heads_per_vreg + e_i])
            norm.append(sc)
        o = o * jnp.concatenate(norm, axis=1)
        ycat_ref[rows, mix_w:] = (o * gate_ref[rows, mix_w:].astype(F32)).astype(BF16)

    for b in range(NB // 2):
        attend(b)

    z = jnp.dot(jnp.concatenate(y_act, axis=1), wglu_ref[...], preferred_element_type=F32)
    y_ssm = z[:, 0:ssm_w] * _sigmoid(z[:, ssm_w:])
    for j in range(n_ssm):
        utb_ref[n_pool + j, cur_rows, :] = y_ssm[:, j * LANES:(j + 1) * LANES]

    for b in range(NB // 2, NB):
        attend(b)

    for c in range(NB // BC):
        crow = slice(c * BC * TT, (c + 1) * BC * TT)
        for bb in range(BC):
            b = c * BC + bb
            rows = slice(b * TT, (b + 1) * TT)
            for k in range(n_slab):
                cols = slice(k * LANES, (k + 1) * LANES)
                yk = utb_ref[k, pl.ds(HALO * NB + b, TT, stride=NB), :]
                ycat_ref[rows, cols] = (yk * gate_ref[rows, cols].astype(F32)).astype(BF16)
        out = jnp.dot(ycat_ref[crow, :], wout_s[...], preferred_element_type=F32)
        ms2 = jnp.mean(out * out, axis=-1, keepdims=True)
        res = x_ref[c * BC:(c + 1) * BC].reshape(BC * TT, d) + (out * lax.rsqrt(ms2 + EPS)) * gpost_ref[...]
        out_ref[c * BC:(c + 1) * BC] = res.reshape(BC, TT, d)


def _const_spec(shape):
    zeros = (0,) * len(shape)
    return pl.BlockSpec(shape, lambda bi, ti: zeros, pipeline_mode=pl.Buffered(1))


def _layer_call(x, kt, v, g_pre, w_in, wpool_bd, pscale, inv_w, inv_c0, wtb, wc, a_mat, d_skip, w_glu, w_out, g_post,
                *, pool_w, ssm_w, att_w, n_mem):
    bsz, seq, d = x.shape
    assert bsz % NB == 0 and seq % TT == 0 and TT % TB == 0 and TT % TC == 0 and TB == HALO and NB % BC == 0
    assert pool_w % LANES == 0 and ssm_w % LANES == 0 and att_w % LANES == 0 and att_w == n_mem
    n_pool, n_ssm = pool_w // LANES, ssm_w // LANES
    n_slab = n_pool + n_ssm
    n_state = a_mat.shape[-1]
    m_rows = NB * TT
    kern = functools.partial(_layer_kernel, d=d, pool_w=pool_w, ssm_w=ssm_w, att_w=att_w, n_mem=n_mem)
    consts = (g_pre, w_in, wpool_bd, pscale, inv_w, inv_c0, wtb, wc, a_mat, d_skip, w_glu, w_out, g_post)
    in_hbm = (w_in, w_out)
    scratch = [
        ((n_slab, (HALO + TT) * NB, LANES), F32),
        ((m_rows, d), BF16),
        ((m_rows, att_w), BF16),
        ((m_rows, d), BF16),
        ((m_rows, pool_w), BF16),
        ((n_ssm, m_rows // TC, n_state), F32),
        ((n_ssm, m_rows // TC, n_state), F32),
        ((n_ssm, NB, n_state), F32),
        ((m_rows, MEM_HEADS * n_mem), F32),
        (w_in.shape, BF16),
        (w_out.shape, BF16),
        ((NB, att_w, MEM_HEADS * n_mem), FP8),
        ((NB, MEM_HEADS * n_mem, att_w), FP8),
        ((NB, att_w, n_mem), BF16),
    ]
    nbytes = lambda shape, dt: math.prod(shape) * jnp.dtype(dt).itemsize
    declared = (sum(nbytes(s, dt) for s, dt in scratch) + 2 * 2 * nbytes((NB, TT, d), x.dtype)
                + sum(nbytes(c.shape, c.dtype) for c in consts if not any(c is h for h in in_hbm)))
    vmem_limit = min(declared + VMEM_SPILL_ALLOWANCE, V7X_VMEM_BYTES - VMEM_RESERVE)
    return pl.pallas_call(
        kern,
        grid=(bsz // NB, seq // TT),
        in_specs=[
            pl.BlockSpec((NB, TT, d), lambda bi, ti: (bi, ti, 0)),
            pl.BlockSpec(memory_space=pl.ANY),
            pl.BlockSpec(memory_space=pl.ANY),
        ] + [pl.BlockSpec(memory_space=pl.ANY) if any(c is h for h in in_hbm) else _const_spec(c.shape)
             for c in consts],
        out_specs=pl.BlockSpec((NB, TT, d), lambda bi, ti: (bi, ti, 0)),
        out_shape=jax.ShapeDtypeStruct(x.shape, x.dtype),
        scratch_shapes=[pltpu.VMEM(s, dt) for s, dt in scratch],
        compiler_params=pltpu.CompilerParams(
            dimension_semantics=("arbitrary", "arbitrary"),
            vmem_limit_bytes=vmem_limit,
        ),
        name="layer_call",
    )(x, kt, v, *consts)


def _expand_kernel(xt_ref, xb_ref, xc_ref, wtb_ref, wc_ref, *, c, n_st):
    gpb = LANES // c
    tl = TC * LANES

    def rep(rows, cols, row_blk, col_blk, period):
        r = lax.broadcasted_iota(jnp.int32, (rows, cols), 0)
        q = lax.broadcasted_iota(jnp.int32, (rows, cols), 1)
        hit = (r // row_blk == q // col_blk) & (r % period == q % period)
        return jnp.where(hit, 1.0, 0.0).astype(BF16)

    def same_group(rows, cols, row_div, col_div):
        r = lax.broadcasted_iota(jnp.int32, (rows, cols), 0)
        q = lax.broadcasted_iota(jnp.int32, (rows, cols), 1)
        return (r // row_div) % gpb == (q // col_div) % gpb

    rep_t = rep(TC * c, tl, c, LANES, c)
    rep_b = rep(2 * n_st, 2 * gpb * n_st, n_st, gpb * n_st, n_st)
    wt = jnp.dot(xt_ref[0].astype(BF16), rep_t, preferred_element_type=F32)
    wtb_ref[0, :, 0:tl] = jnp.where(same_group(tl, tl, c, c), wt, 0.0).astype(BF16)
    wb = jnp.dot(xb_ref[0].astype(BF16), rep_b, preferred_element_type=F32)
    wtb_ref[0, :, tl:] = jnp.where(same_group(tl, 2 * gpb * n_st, c, n_st), wb, 0.0).astype(BF16)
    wcx = jnp.dot(xc_ref[0].astype(BF16), rep_t, preferred_element_type=F32)
    wc_ref[0] = jnp.where(same_group(2 * gpb * n_st, tl, n_st, c), wcx, 0.0).astype(BF16)


def _ssm_matrices(a_re, a_im, log_dt, b_re, b_im, c_re, c_im):
    n_grp, n_st = a_re.shape
    c = b_re.shape[-1]
    gpb = LANES // c
    nj = n_grp // gpb
    dt = jnp.exp(log_dt)[:, None]
    ks = jnp.arange(TC + 1, dtype=F32)[:, None, None]
    mag = jnp.exp(ks * (a_re * dt)[None])
    ang = ks * (a_im * dt)[None]
    p_re, p_im = mag * jnp.cos(ang), mag * jnp.sin(ang)
    den = a_re * a_re + a_im * a_im
    x_re, x_im = p_re[1] - 1.0, p_im[1]
    f_re = (x_re * a_re + x_im * a_im) / den
    f_im = (x_im * a_re - x_re * a_im) / den
    bt_re, bt_im = jnp.swapaxes(b_re, 1, 2), jnp.swapaxes(b_im, 1, 2)
    bb_re = f_re[:, None, :] * bt_re - f_im[:, None, :] * bt_im
    bb_im = f_re[:, None, :] * bt_im + f_im[:, None, :] * bt_re
    cp_re = c_re[None] * p_re[:, :, None, :] - c_im[None] * p_im[:, :, None, :]
    cp_im = c_re[None] * p_im[:, :, None, :] + c_im[None] * p_re[:, :, None, :]
    kern = jnp.sum(bb_re[None, :, :, None, :] * cp_re[:TC, :, None, :, :]
                   - bb_im[None, :, :, None, :] * cp_im[:TC, :, None, :, :], axis=-1)
    zero = jnp.zeros_like(kern[0])
    xt = jnp.stack([jnp.concatenate([kern[t - s] if t >= s else zero for t in range(TC)], axis=-1)
                    for s in range(TC)], axis=0)
    xt = xt.reshape(TC, nj, gpb * c, TC * c).swapaxes(0, 1).reshape(nj, TC * LANES, TC * c)
    q_re, q_im = p_re[TC - 1::-1][:, :, None, :], p_im[TC - 1::-1][:, :, None, :]
    xb = jnp.concatenate([q_re * bb_re[None] - q_im * bb_im[None], q_re * bb_im[None] + q_im * bb_re[None]], axis=-1)
    xb = xb.reshape(TC, nj, gpb * c, 2 * n_st).swapaxes(0, 1).reshape(nj, TC * LANES, 2 * n_st)
    xc = jnp.stack([cp_re[1:], -cp_im[1:]], axis=0)
    xc = jnp.transpose(xc, (2, 0, 4, 1, 3)).reshape(nj, gpb, 2, n_st, TC * c)
    xc = xc.swapaxes(1, 2).reshape(nj, 2 * gpb * n_st, TC * c)
    a_mat = jnp.concatenate([p_re[TC].reshape(nj, gpb * n_st), p_im[TC].reshape(nj, gpb * n_st)], axis=-1)
    a_mat = jnp.broadcast_to(a_mat[:, None, :], (nj, NB, a_mat.shape[-1]))
    n_state = 2 * gpb * n_st
    wtb, wc = pl.pallas_call(
        functools.partial(_expand_kernel, c=c, n_st=n_st),
        grid=(nj,),
        in_specs=[pl.BlockSpec((1,) + xt.shape[1:], lambda j: (j, 0, 0)),
                  pl.BlockSpec((1,) + xb.shape[1:], lambda j: (j, 0, 0)),
                  pl.BlockSpec((1,) + xc.shape[1:], lambda j: (j, 0, 0))],
        out_specs=[pl.BlockSpec((1, TC * LANES, TC * LANES + n_state), lambda j: (j, 0, 0)),
                   pl.BlockSpec((1, n_state, TC * LANES), lambda j: (j, 0, 0))],
        out_shape=[jax.ShapeDtypeStruct((nj, TC * LANES, TC * LANES + n_state), BF16),
                   jax.ShapeDtypeStruct((nj, n_state, TC * LANES), BF16)],
        compiler_params=pltpu.CompilerParams(dimension_semantics=("arbitrary",)),
        name="ssm_expand_call",
    )(xt, xb, xc)
    return wtb, wc, a_mat


def _pool_constants(w_pool, pool_w):
    n_grp, gw, _ = w_pool.shape
    wbd = (w_pool[:, :, None, :] * jnp.eye(n_grp, dtype=F32)[:, None, :, None]).reshape(pool_w, pool_w)
    win = np.repeat(np.asarray(POOL_WINDOWS, np.float32), gw)
    inv_w = jnp.asarray((1.0 / win)[None, :])
    pos = np.repeat(np.arange(1, HALO + 1, dtype=np.float32), NB)[:, None]
    inv_c0 = jnp.asarray(1.0 / np.minimum(pos, win[None, :]))
    return wbd.astype(BF16), inv_w, inv_c0


def _layer(x, mem, g_pre, w_in, w_pool, pool_scale, a_re, a_im, log_dt, b_re, b_im, c_re, c_im, d_skip, w_glu,
           g_mem, w_kv, w_out, g_post):
    d = x.shape[-1]
    n_mem = mem.shape[1]
    pool_w = w_pool.shape[0] * w_pool.shape[1]
    ssm_w = d_skip.shape[0]
    att_w = w_kv.shape[1] // 2
    assert a_re.shape[0] * SSM_GROUP == ssm_w and b_re.shape[-1] == SSM_GROUP
    assert pool_w + ssm_w + att_w == d and w_pool.shape[0] == len(POOL_WINDOWS)

    kt, v = _kv_call(mem, g_mem[None, :], w_kv[:, :att_w].astype(BF16), w_kv[:, att_w:].astype(BF16))
    wtb, wc, a_mat = _ssm_matrices(a_re, a_im, log_dt, b_re, b_im, c_re, c_im)
    wpool_bd, inv_w, inv_c0 = _pool_constants(w_pool, pool_w)
    return _layer_call(
        x, kt, v, g_pre[None, :], w_in.astype(BF16), wpool_bd, pool_scale[None, :], inv_w, inv_c0,
        wtb, wc, a_mat, d_skip[None, :], w_glu.astype(BF16), w_out.astype(BF16), g_post[None, :],
        pool_w=pool_w, ssm_w=ssm_w, att_w=att_w, n_mem=n_mem)


def kernel(x, mem, g_pre, w_in, w_pool, pool_scale, a_re, a_im, log_dt, b_re, b_im, c_re, c_im, d_skip, w_glu, g_mem, w_kv, w_out, g_post):
    for i in range(g_pre.shape[0]):
        x = _layer(x, mem, g_pre[i], w_in[i], w_pool[i], pool_scale[i], a_re[i], a_im[i], log_dt[i], b_re[i], b_im[i],
                   c_re[i], c_im[i], d_skip[i], w_glu[i], g_mem[i], w_kv[i], w_out[i], g_post[i])
    return x
```

```python
import functools
import math

import jax
import jax.numpy as jnp
import numpy as np
from jax import lax
from jax.experimental import pallas as pl
from jax.experimental.pallas import tpu as pltpu

LANES = 128
NB = 8
TT = 128
TC = 4
BC = 4
TB = 16
KV_MB = 8
HALO = 16
POOL_WINDOWS = (2, 4, 8, 16)
SSM_GROUP = 16
MEM_HEADS = 4
EPS = 1e-6
LOG2_E = 1.4426950408889634
GELU_C = 0.7978845608028654
V7X_VMEM_BYTES = 64 * 1024 * 1024
VMEM_RESERVE = 4 * 1024 * 1024
VMEM_SPILL_ALLOWANCE = 8 * 1024 * 1024

F32 = jnp.float32
BF16 = jnp.bfloat16
FP8 = jnp.float8_e4m3fn


def _sigmoid(v):
    return 0.5 * jnp.tanh(0.5 * v) + 0.5


def _silu(v):
    hv = 0.5 * v
    return hv + hv * jnp.tanh(hv)


def _gelu_tanh(v):
    hv = 0.5 * v
    return hv + hv * jnp.tanh(v * (GELU_C + (GELU_C * 0.044715) * (v * v)))


def _kv_kernel(mem_ref, gmem_ref, wk_ref, wv_ref, kt_ref, v_ref, *, att_w, n_mem):
    mb, _, d = mem_ref.shape
    m = mem_ref[...].reshape(mb * n_mem, d)
    ms = jnp.mean(m * m, axis=-1, keepdims=True)
    mn = ((m * lax.rsqrt(ms + EPS)) * gmem_ref[...]).astype(BF16)
    kt = lax.dot_general(wk_ref[...], mn, (((0,), (1,)), ((), ())), preferred_element_type=F32)
    v = jnp.dot(mn, wv_ref[...], preferred_element_type=F32)
    hd = att_w // MEM_HEADS
    kt = kt * (hd ** -0.5 * LOG2_E)
    for i in range(mb):
        kt_ref[i] = kt[:, i * n_mem:(i + 1) * n_mem].astype(BF16)
        v_ref[i] = v[i * n_mem:(i + 1) * n_mem, :].astype(BF16)


def _kv_call(mem, g_mem, wk, wv):
    bsz, n_mem, d = mem.shape
    att_w = wv.shape[1]
    assert bsz % KV_MB == 0
    kern = functools.partial(_kv_kernel, att_w=att_w, n_mem=n_mem)
    return pl.pallas_call(
        kern,
        grid=(bsz // KV_MB,),
        in_specs=[
            pl.BlockSpec((KV_MB, n_mem, d), lambda b: (b, 0, 0)),
            pl.BlockSpec((1, d), lambda b: (0, 0)),
            pl.BlockSpec((d, att_w), lambda b: (0, 0)),
            pl.BlockSpec((d, att_w), lambda b: (0, 0)),
        ],
        out_specs=[
            pl.BlockSpec((KV_MB, att_w, n_mem), lambda b: (b, 0, 0)),
            pl.BlockSpec((KV_MB, n_mem, att_w), lambda b: (b, 0, 0)),
        ],
        out_shape=[
            jax.ShapeDtypeStruct((bsz, att_w, n_mem), BF16),
            jax.ShapeDtypeStruct((bsz, n_mem, att_w), BF16),
        ],
        compiler_params=pltpu.CompilerParams(dimension_semantics=("arbitrary",)),
        name="kv_call",
    )(mem, g_mem, wk, wv)


def _layer_kernel(x_ref, kt_ref, v_ref, gpre_ref, win_ref, wpool_ref, pscale_ref, invw_ref, invc0_ref,
                  wtb_ref, wc_ref, a_ref, dskip_ref, wglu_ref, wout_ref, gpost_ref,
                  out_ref,
                  utb_ref, gate_ref, q_ref, ycat_ref, pd_ref, g_ref, hin_ref, hc_ref,
                  s_ref, win_s, wout_s, kbd_s, vbd_s, kv_stage,
                  *, d, pool_w, ssm_w, att_w, n_mem):
    bi = pl.program_id(0)
    ti = pl.program_id(1)
    n_pool = pool_w // LANES
    n_ssm = ssm_w // LANES
    n_slab = n_pool + n_ssm
    mix_w = pool_w + ssm_w
    n_chunk = TT // TC
    st = hc_ref.shape[-1] // 2
    hd = att_w // MEM_HEADS
    cur_rows = slice(HALO * NB, (HALO + TT) * NB)

    @pl.when((bi == 0) & (ti == 0))
    def _():
        pltpu.sync_copy(win_ref, win_s)
        pltpu.sync_copy(wout_ref, wout_s)

    @pl.when(ti == 0)
    def _():
        group = pl.ds(pl.multiple_of(bi * NB, NB), NB)
        row_head = lax.broadcasted_iota(jnp.int32, (att_w, n_mem), 0) // hd
        col_head = lax.broadcasted_iota(jnp.int32, (n_mem, att_w), 1) // hd
        pltpu.sync_copy(kt_ref.at[group], kv_stage)
        for b in range(NB):
            kt_b = kv_stage[b].astype(F32)
            for hh in range(MEM_HEADS):
                kbd_s[b, :, hh * n_mem:(hh + 1) * n_mem] = jnp.where(row_head == hh, kt_b, 0.0).astype(FP8)
        pltpu.sync_copy(v_ref.at[group], kv_stage)
        for b in range(NB):
            v_b = kv_stage[b].astype(F32)
            for hh in range(MEM_HEADS):
                vbd_s[b, hh * n_mem:(hh + 1) * n_mem, :] = jnp.where(col_head == hh, v_b, 0.0).astype(FP8)
        utb_ref[0:n_pool, 0:HALO * NB, :] = jnp.zeros((n_pool, HALO * NB, LANES), F32)
        hc_ref[...] = jnp.zeros(hc_ref.shape, F32)

    for c in range(NB // BC):
        crow = slice(c * BC * TT, (c + 1) * BC * TT)
        x = x_ref[c * BC:(c + 1) * BC].reshape(BC * TT, d)
        ms = jnp.mean(x * x, axis=-1, keepdims=True)
        h = ((x * lax.rsqrt(ms + EPS)) * gpre_ref[...]).astype(BF16)
        val = jnp.dot(h, win_s[:, 0:mix_w], preferred_element_type=F32)
        for bb in range(BC):
            for k in range(n_slab):
                utb_ref[k, pl.ds(HALO * NB + c * BC + bb, TT, stride=NB), :] = (
                    val[bb * TT:(bb + 1) * TT, k * LANES:(k + 1) * LANES])
        q_ref[crow, :] = jnp.dot(h, win_s[:, mix_w:mix_w + att_w], preferred_element_type=F32).astype(BF16)
        gate = jnp.dot(h, win_s[:, mix_w + att_w:], preferred_element_type=F32)
        gate_ref[crow, :] = _silu(gate).astype(BF16)

    y_intra = []
    for j in range(n_ssm):
        u = utb_ref[n_pool + j, cur_rows, :]
        u4 = u.reshape(n_chunk, TC, NB, LANES)
        up = jnp.concatenate([u4[:, s].reshape(n_chunk * NB, LANES) for s in range(TC)], axis=1).astype(BF16)
        yg = jnp.dot(up, wtb_ref[j], preferred_element_type=F32)
        y_intra.append(yg[:, 0:TC * LANES])
        g_ref[j] = yg[:, TC * LANES:]

    for b in range(NB):
        rows = slice(b * TT, (b + 1) * TT)
        s_ref[rows, :] = jnp.dot(q_ref[rows, :], kbd_s[b], preferred_element_type=F32)

    gw = pool_w // len(POOL_WINDOWS)
    lane = lax.broadcasted_iota(jnp.int32, (TB * NB, LANES), 1)
    for k in range(n_pool):
        w_lo = POOL_WINDOWS[(k * LANES) // gw]
        w_hi = POOL_WINDOWS[min(((k + 1) * LANES - 1) // gw, len(POOL_WINDOWS) - 1)]
        split = ((k * LANES) // gw + 1) * gw - k * LANES
        for i in range(TT // TB):
            base = (HALO + i * TB) * NB
            ext = utb_ref[k, base - (HALO - 1) * NB: base + TB * NB, :]
            sums = {1: ext}
            w = 1
            while w < w_hi:
                prev = sums[w]
                sums[2 * w] = prev[w * NB:] + prev[:-w * NB]
                w *= 2
            cur = ext[(HALO - 1) * NB:]
            lo = sums[w_lo][(HALO - w_lo) * NB:]
            hi = sums[w_hi][(HALO - w_hi) * NB:]
            ssum = jnp.where(lane < split, lo, hi)
            scale = invw_ref[:, k * LANES:(k + 1) * LANES]
            if i == 0:
                scale = jnp.where(ti == 0, invc0_ref[:, k * LANES:(k + 1) * LANES], scale)
            pd_ref[i * TB * NB:(i + 1) * TB * NB, k * LANES:(k + 1) * LANES] = (ssum * scale - cur).astype(BF16)
    for k in range(n_pool):
        utb_ref[k, 0:HALO * NB, :] = utb_ref[k, TT * NB:(TT + HALO) * NB, :]
    ypool = jnp.dot(pd_ref[...], wpool_ref[...], preferred_element_type=F32) * pscale_ref[...]
    for k in range(n_pool):
        utb_ref[k, cur_rows, :] = ypool[:, k * LANES:(k + 1) * LANES]

    a_re = [a_ref[j, :, 0:st] for j in range(n_ssm)]
    a_im = [a_ref[j, :, st:] for j in range(n_ssm)]
    h_re = [hc_ref[j, :, 0:st] for j in range(n_ssm)]
    h_im = [hc_ref[j, :, st:] for j in range(n_ssm)]
    for r in range(n_chunk):
        rows = slice(r * NB, (r + 1) * NB)
        for j in range(n_ssm):
            hin_ref[j, rows, 0:st] = h_re[j]
            hin_ref[j, rows, st:] = h_im[j]
            g_re = g_ref[j, rows, 0:st]
            g_im = g_ref[j, rows, st:]
            h_re[j], h_im[j] = (a_re[j] * h_re[j] - a_im[j] * h_im[j] + g_re,
                                a_re[j] * h_im[j] + a_im[j] * h_re[j] + g_im)
    for j in range(n_ssm):
        hc_ref[j, :, 0:st] = h_re[j]
        hc_ref[j, :, st:] = h_im[j]

    y_act = []
    for j in range(n_ssm):
        y_lin = y_intra[j] + jnp.dot(hin_ref[j].astype(BF16), wc_ref[j], preferred_element_type=F32)
        y_tb = jnp.stack([y_lin[:, s * LANES:(s + 1) * LANES].reshape(n_chunk, NB, LANES) for s in range(TC)],
                         axis=1).reshape(TT * NB, LANES)
        y = y_tb + dskip_ref[:, j * LANES:(j + 1) * LANES] * utb_ref[n_pool + j, cur_rows, :]
        y_act.append(_gelu_tanh(y).astype(BF16))

    lane_t = lax.broadcasted_iota(jnp.int32, (TT, LANES), 1)
    heads_per_vreg = LANES // hd

    def attend(b):
        rows = slice(b * TT, (b + 1) * TT)
        es, inv = [], []
        for hh in range(MEM_HEADS):
            sh = s_ref[rows, hh * n_mem:(hh + 1) * n_mem]
            e = jnp.exp2((sh - jnp.max(sh, axis=-1, keepdims=True)).astype(BF16))
            inv.append(1.0 / jnp.sum(e.astype(F32), axis=-1, keepdims=True))
            es.append(e)
        o = jnp.dot(jnp.concatenate(es, axis=1), vbd_s[b], preferred_element_type=F32)
        norm = []
        for c in range(att_w // LANES):
            sc = inv[c * heads_per_vreg]
            for e_i in range(1, heads_per_vreg):
                sc = jnp.where(lane_t < e_i * hd, sc, inv[c * heads_per_vreg + e_i])
            norm.append(sc)
        o = o * jnp.concatenate(norm, axis=1)
        ycat_ref[rows, mix_w:] = (o * gate_ref[rows, mix_w:].astype(F32)).astype(BF16)

    for b in range(NB - 2):
        attend(b)

    z = jnp.dot(jnp.concatenate(y_act, axis=1), wglu_ref[...], preferred_element_type=F32)
    y_ssm = z[:, 0:ssm_w] * _sigmoid(z[:, ssm_w:])
    for j in range(n_ssm):
        utb_ref[n_pool + j, cur_rows, :] = y_ssm[:, j * LANES:(j + 1) * LANES]

    for b in range(NB - 2, NB):
        attend(b)

    for c in range(NB // BC):
        crow = slice(c * BC * TT, (c + 1) * BC * TT)
        for bb in range(BC):
            b = c * BC + bb
            rows = slice(b * TT, (b + 1) * TT)
            for k in range(n_slab):
                cols = slice(k * LANES, (k + 1) * LANES)
                yk = utb_ref[k, pl.ds(HALO * NB + b, TT, stride=NB), :]
                ycat_ref[rows, cols] = (yk * gate_ref[rows, cols].astype(F32)).astype(BF16)
        out = jnp.dot(ycat_ref[crow, :], wout_s[...], preferred_element_type=F32)
        ms2 = jnp.mean(out * out, axis=-1, keepdims=True)
        res = x_ref[c * BC:(c + 1) * BC].reshape(BC * TT, d) + (out * lax.rsqrt(ms2 + EPS)) * gpost_ref[...]
        out_ref[c * BC:(c + 1) * BC] = res.reshape(BC, TT, d)


def _const_spec(shape):
    zeros = (0,) * len(shape)
    return pl.BlockSpec(shape, lambda bi, ti: zeros, pipeline_mode=pl.Buffered(1))


def _layer_call(x, kt, v, g_pre, w_in, wpool_bd, pscale, inv_w, inv_c0, wtb, wc, a_mat, d_skip, w_glu, w_out, g_post,
                *, pool_w, ssm_w, att_w, n_mem):
    bsz, seq, d = x.shape
    assert bsz % NB == 0 and seq % TT == 0 and TT % TB == 0 and TT % TC == 0 and TB == HALO and NB % BC == 0
    assert pool_w % LANES == 0 and ssm_w % LANES == 0 and att_w % LANES == 0 and att_w == n_mem
    n_pool, n_ssm = pool_w // LANES, ssm_w // LANES
    n_slab = n_pool + n_ssm
    n_state = a_mat.shape[-1]
    m_rows = NB * TT
    kern = functools.partial(_layer_kernel, d=d, pool_w=pool_w, ssm_w=ssm_w, att_w=att_w, n_mem=n_mem)
    consts = (g_pre, w_in, wpool_bd, pscale, inv_w, inv_c0, wtb, wc, a_mat, d_skip, w_glu, w_out, g_post)
    in_hbm = (w_in, w_out)
    scratch = [
        ((n_slab, (HALO + TT) * NB, LANES), F32),
        ((m_rows, d), BF16),
        ((m_rows, att_w), BF16),
        ((m_rows, d), BF16),
        ((m_rows, pool_w), BF16),
        ((n_ssm, m_rows // TC, n_state), F32),
        ((n_ssm, m_rows // TC, n_state), F32),
        ((n_ssm, NB, n_state), F32),
        ((m_rows, MEM_HEADS * n_mem), F32),
        (w_in.shape, BF16),
        (w_out.shape, BF16),
        ((NB, att_w, MEM_HEADS * n_mem), FP8),
        ((NB, MEM_HEADS * n_mem, att_w), FP8),
        ((NB, att_w, n_mem), BF16),
    ]
    nbytes = lambda shape, dt: math.prod(shape) * jnp.dtype(dt).itemsize
    declared = (sum(nbytes(s, dt) for s, dt in scratch) + 2 * 2 * nbytes((NB, TT, d), x.dtype)
                + sum(nbytes(c.shape, c.dtype) for c in consts if not any(c is h for h in in_hbm)))
    vmem_limit = min(declared + VMEM_SPILL_ALLOWANCE, V7X_VMEM_BYTES - VMEM_RESERVE)
    return pl.pallas_call(
        kern,
        grid=(bsz // NB, seq // TT),
        in_specs=[
            pl.BlockSpec((NB, TT, d), lambda bi, ti: (bi, ti, 0)),
            pl.BlockSpec(memory_space=pl.ANY),
            pl.BlockSpec(memory_space=pl.ANY),
        ] + [pl.BlockSpec(memory_space=pl.ANY) if any(c is h for h in in_hbm) else _const_spec(c.shape)
             for c in consts],
        out_specs=pl.BlockSpec((NB, TT, d), lambda bi, ti: (bi, ti, 0)),
        out_shape=jax.ShapeDtypeStruct(x.shape, x.dtype),
        scratch_shapes=[pltpu.VMEM(s, dt) for s, dt in scratch],
        compiler_params=pltpu.CompilerParams(
            dimension_semantics=("arbitrary", "arbitrary"),
            vmem_limit_bytes=vmem_limit,
        ),
        name="layer_call",
    )(x, kt, v, *consts)


def _expand_kernel(xt_ref, xb_ref, xc_ref, wtb_ref, wc_ref, *, c, n_st):
    gpb = LANES // c
    tl = TC * LANES

    def rep(rows, cols, row_blk, col_blk, period):
        r = lax.broadcasted_iota(jnp.int32, (rows, cols), 0)
        q = lax.broadcasted_iota(jnp.int32, (rows, cols), 1)
        hit = (r // row_blk == q // col_blk) & (r % period == q % period)
        return jnp.where(hit, 1.0, 0.0).astype(BF16)

    def same_group(rows, cols, row_div, col_div):
        r = lax.broadcasted_iota(jnp.int32, (rows, cols), 0)
        q = lax.broadcasted_iota(jnp.int32, (rows, cols), 1)
        return (r // row_div) % gpb == (q // col_div) % gpb

    rep_t = rep(TC * c, tl, c, LANES, c)
    rep_b = rep(2 * n_st, 2 * gpb * n_st, n_st, gpb * n_st, n_st)
    wt = jnp.dot(xt_ref[0].astype(BF16), rep_t, preferred_element_type=F32)
    wtb_ref[0, :, 0:tl] = jnp.where(same_group(tl, tl, c, c), wt, 0.0).astype(BF16)
    wb = jnp.dot(xb_ref[0].astype(BF16), rep_b, preferred_element_type=F32)
    wtb_ref[0, :, tl:] = jnp.where(same_group(tl, 2 * gpb * n_st, c, n_st), wb, 0.0).astype(BF16)
    wcx = jnp.dot(xc_ref[0].astype(BF16), rep_t, preferred_element_type=F32)
    wc_ref[0] = jnp.where(same_group(2 * gpb * n_st, tl, n_st, c), wcx, 0.0).astype(BF16)


def _ssm_matrices(a_re, a_im, log_dt, b_re, b_im, c_re, c_im):
    n_grp, n_st = a_re.shape
    c = b_re.shape[-1]
    gpb = LANES // c
    nj = n_grp // gpb
    dt = jnp.exp(log_dt)[:, None]
    ks = jnp.arange(TC + 1, dtype=F32)[:, None, None]
    mag = jnp.exp(ks * (a_re * dt)[None])
    ang = ks * (a_im * dt)[None]
    p_re, p_im = mag * jnp.cos(ang), mag * jnp.sin(ang)
    den = a_re * a_re + a_im * a_im
    x_re, x_im = p_re[1] - 1.0, p_im[1]
    f_re = (x_re * a_re + x_im * a_im) / den
    f_im = (x_im * a_re - x_re * a_im) / den
    bt_re, bt_im = jnp.swapaxes(b_re, 1, 2), jnp.swapaxes(b_im, 1, 2)
    bb_re = f_re[:, None, :] * bt_re - f_im[:, None, :] * bt_im
    bb_im = f_re[:, None, :] * bt_im + f_im[:, None, :] * bt_re
    cp_re = c_re[None] * p_re[:, :, None, :] - c_im[None] * p_im[:, :, None, :]
    cp_im = c_re[None] * p_im[:, :, None, :] + c_im[None] * p_re[:, :, None, :]
    kern = jnp.sum(bb_re[None, :, :, None, :] * cp_re[:TC, :, None, :, :]
                   - bb_im[None, :, :, None, :] * cp_im[:TC, :, None, :, :], axis=-1)
    zero = jnp.zeros_like(kern[0])
    xt = jnp.stack([jnp.concatenate([kern[t - s] if t >= s else zero for t in range(TC)], axis=-1)
                    for s in range(TC)], axis=0)
    xt = xt.reshape(TC, nj, gpb * c, TC * c).swapaxes(0, 1).reshape(nj, TC * LANES, TC * c)
    q_re, q_im = p_re[TC - 1::-1][:, :, None, :], p_im[TC - 1::-1][:, :, None, :]
    xb = jnp.concatenate([q_re * bb_re[None] - q_im * bb_im[None], q_re * bb_im[None] + q_im * bb_re[None]], axis=-1)
    xb = xb.reshape(TC, nj, gpb * c, 2 * n_st).swapaxes(0, 1).reshape(nj, TC * LANES, 2 * n_st)
    xc = jnp.stack([cp_re[1:], -cp_im[1:]], axis=0)
    xc = jnp.transpose(xc, (2, 0, 4, 1, 3)).reshape(nj, gpb, 2, n_st, TC * c)
    xc = xc.swapaxes(1, 2).reshape(nj, 2 * gpb * n_st, TC * c)
    a_mat = jnp.concatenate([p_re[TC].reshape(nj, gpb * n_st), p_im[TC].reshape(nj, gpb * n_st)], axis=-1)
    a_mat = jnp.broadcast_to(a_mat[:, None, :], (nj, NB, a_mat.shape[-1]))
    n_state = 2 * gpb * n_st
    wtb, wc = pl.pallas_call(
        functools.partial(_expand_kernel, c=c, n_st=n_st),
        grid=(nj,),
        in_specs=[pl.BlockSpec((1,) + xt.shape[1:], lambda j: (j, 0, 0)),
                  pl.BlockSpec((1,) + xb.shape[1:], lambda j: (j, 0, 0)),
                  pl.BlockSpec((1,) + xc.shape[1:], lambda j: (j, 0, 0))],
        out_specs=[pl.BlockSpec((1, TC * LANES, TC * LANES + n_state), lambda j: (j, 0, 0)),
                   pl.BlockSpec((1, n_state, TC * LANES), lambda j: (j, 0, 0))],
        out_shape=[jax.ShapeDtypeStruct((nj, TC * LANES, TC * LANES + n_state), BF16),
                   jax.ShapeDtypeStruct((nj, n_state, TC * LANES), BF16)],
        compiler_params=pltpu.CompilerParams(dimension_semantics=("arbitrary",)),
        name="ssm_expand_call",
    )(xt, xb, xc)
    return wtb, wc, a_mat


def _pool_constants(w_pool, pool_w):
    n_grp, gw, _ = w_pool.shape
    wbd = (w_pool[:, :, None, :] * jnp.eye(n_grp, dtype=F32)[:, None, :, None]).reshape(pool_w, pool_w)
    win = np.repeat(np.asarray(POOL_WINDOWS, np.float32), gw)
    inv_w = jnp.asarray((1.0 / win)[None, :])
    pos = np.repeat(np.arange(1, HALO + 1, dtype=np.float32), NB)[:, None]
    inv_c0 = jnp.asarray(1.0 / np.minimum(pos, win[None, :]))
    return wbd.astype(BF16), inv_w, inv_c0


def _layer(x, mem, g_pre, w_in, w_pool, pool_scale, a_re, a_im, log_dt, b_re, b_im, c_re, c_im, d_skip, w_glu,
           g_mem, w_kv, w_out, g_post):
    d = x.shape[-1]
    n_mem = mem.shape[1]
    pool_w = w_pool.shape[0] * w_pool.shape[1]
    ssm_w = d_skip.shape[0]
    att_w = w_kv.shape[1] // 2
    assert a_re.shape[0] * SSM_GROUP == ssm_w and b_re.shape[-1] == SSM_GROUP
    assert pool_w + ssm_w + att_w == d and w_pool.shape[0] == len(POOL_WINDOWS)

    kt, v = _kv_call(mem, g_mem[None, :], w_kv[:, :att_w].astype(BF16), w_kv[:, att_w:].astype(BF16))
    wtb, wc, a_mat = _ssm_matrices(a_re, a_im, log_dt, b_re, b_im, c_re, c_im)
    wpool_bd, inv_w, inv_c0 = _pool_constants(w_pool, pool_w)
    return _layer_call(
        x, kt, v, g_pre[None, :], w_in.astype(BF16), wpool_bd, pool_scale[None, :], inv_w, inv_c0,
        wtb, wc, a_mat, d_skip[None, :], w_glu.astype(BF16), w_out.astype(BF16), g_post[None, :],
        pool_w=pool_w, ssm_w=ssm_w, att_w=att_w, n_mem=n_mem)


def kernel(x, mem, g_pre, w_in, w_pool, pool_scale, a_re, a_im, log_dt, b_re, b_im, c_re, c_im, d_skip, w_glu, g_mem, w_kv, w_out, g_post):
    for i in range(g_pre.shape[0]):
        x = _layer(x, mem, g_pre[i], w_in[i], w_pool[i], pool_scale[i], a_re[i], a_im[i], log_dt[i], b_re[i], b_im[i],
                   c_re[i], c_im[i], d_skip[i], w_glu[i], g_mem[i], w_kv[i], w_out[i], g_post[i])
    return x
```
